```python
import jax
import jax.numpy as jnp
from jax import lax
import numpy as np

D_MODEL = 2048
BATCH = 2
SEQ = 16384
DEPTH = 4

RWKV_HEADS = 16
RWKV_HEAD = 64
RWKV_W = RWKV_HEADS * RWKV_HEAD
DECAY_LORA = 64
AAA_LORA = 64
MV_LORA = 32
GATE_LORA = 160
RWKV_GN_EPS = 64e-5
MLA_HEADS = 8
Q_LORA = 512
KV_LORA = 256
QK_NOPE = 128
QK_ROPE = 64
V_HEAD = 128
MLA_W = MLA_HEADS * V_HEAD
ROPE_THETA = 10000.0
ATTN_SCALE = (QK_NOPE + QK_ROPE) ** -0.5
Q_BLOCK = 128
CONV_W = 1024
CONV_K = 31
N_BRANCH = 3
BRANCH_W = 1024
N_EXPERTS = 32
N_GROUPS = 4
EXP_PER_GROUP = N_EXPERTS // N_GROUPS
TOP_K = 2
D_EXPERT = 512
MOE_BLOCK = 256
DN_ALPHA = (2 * DEPTH) ** 0.25
DN_BETA = (8 * DEPTH) ** -0.25
LN_EPS = 1e-5
RMS_EPS = 1e-6
RW_SHIFT = 3 * RWKV_W + DECAY_LORA + AAA_LORA + GATE_LORA
MLA_IN = Q_LORA + KV_LORA + QK_ROPE
CONV_IN = 2 * CONV_W
GATE_IN = N_BRANCH * D_MODEL
C_IN = RW_SHIFT + MLA_IN + CONV_IN + GATE_IN
IN_SPLITS = (RW_SHIFT, RW_SHIFT + Q_LORA, RW_SHIFT + Q_LORA + KV_LORA, RW_SHIFT + MLA_IN, RW_SHIFT + MLA_IN + CONV_IN)
RW_SPLITS = (RWKV_W, 2 * RWKV_W, 3 * RWKV_W, 3 * RWKV_W + DECAY_LORA, 3 * RWKV_W + DECAY_LORA + AAA_LORA)

kernel_name = 'hybrid_rwkv7_mla_conformer_groupmoe_deepnorm'


def layer_norm(x, g, b, eps=LN_EPS):
    xf = x.astype(jnp.float32)
    mu = jnp.mean(xf, axis=-1, keepdims=True)
    var = jnp.mean(jnp.square(xf - mu), axis=-1, keepdims=True)
    return ((xf - mu) * lax.rsqrt(var + eps)).astype(x.dtype) * g + b


def rms_norm(x, g, eps=RMS_EPS):
    xf = x.astype(jnp.float32)
    return (xf * lax.rsqrt(jnp.mean(xf * xf, axis=-1, keepdims=True) + eps)).astype(x.dtype) * g


def token_shift(z):
    return jnp.pad(z, ((0, 0), (1, 0), (0, 0)))[:, :-1]


def rope_cos_sin(positions):
    inv = ROPE_THETA ** (-jnp.arange(0, QK_ROPE, 2, dtype=jnp.float32) / QK_ROPE)
    ang = positions.astype(jnp.float32)[..., None] * inv
    return jnp.cos(ang), jnp.sin(ang)


def apply_rope(t, cos, sin):
    t1, t2 = jnp.split(t, 2, axis=-1)
    return jnp.concatenate([t1 * cos - t2 * sin, t1 * sin + t2 * cos], axis=-1).astype(t.dtype)


def rwkv7_time_mix(z, v_first, w0, w2, a0, a2, g2, k_k, k_a, r_k, lnx_g, lnx_b, vres):
    B, S, _ = z.shape
    f32 = jnp.float32
    heads = lambda t: t.reshape(B, S, RWKV_HEADS, RWKV_HEAD)
    r, k, v, zw, za, zg = jnp.split(z, RW_SPLITS, axis=-1)
    w = -jax.nn.softplus(-(w0 + jnp.tanh(zw) @ w2)) - 0.5
    decay = jnp.exp(-jnp.exp(w.astype(f32)))
    a = jax.nn.sigmoid(a0 + za @ a2)
    g = jax.nn.sigmoid(zg) @ g2
    if vres is None:
        v_first = v
    else:
        v0, v1, v2 = vres
        v = v + (v_first - v) * jax.nn.sigmoid(v0 + (v @ v1) @ v2)
    kk = heads(k * k_k).astype(f32)
    kk = kk * lax.rsqrt(jnp.maximum(jnp.sum(kk * kk, axis=-1, keepdims=True), 1e-24))
    k = k * (1 + (a - 1) * k_a)
    rh, kh, vh, ah, dh = (heads(t).astype(f32) for t in (r, k, v, a, decay))

    def step(state, inp):
        r_t, d_t, k_t, v_t, kk_t, a_t = inp
        s_kk = jnp.einsum('bhvk,bhk->bhv', state, kk_t)
        state = (state * d_t[:, :, None, :]
                 - s_kk[..., None] * (kk_t * a_t)[:, :, None, :]
                 + v_t[..., None] * k_t[:, :, None, :])
        return state, jnp.einsum('bhvk,bhk->bhv', state, r_t)

    xs = tuple(jnp.moveaxis(t, 1, 0) for t in (rh, dh, kh, vh, kk, ah))
    state0 = jnp.zeros((B, RWKV_HEADS, RWKV_HEAD, RWKV_HEAD), f32)
    _, y = lax.scan(step, state0, xs)
    y = jnp.moveaxis(y, 0, 1)
    mu = jnp.mean(y, axis=-1, keepdims=True)
    var = jnp.mean(jnp.square(y - mu), axis=-1, keepdims=True)
    y = ((y - mu) * lax.rsqrt(var + RWKV_GN_EPS)).reshape(B, S, RWKV_W) * lnx_g.astype(f32) + lnx_b.astype(f32)
    bonus = jnp.sum(rh * kh * r_k.astype(f32).reshape(RWKV_HEADS, RWKV_HEAD), axis=-1, keepdims=True) * vh
    y = (y + bonus.reshape(B, S, RWKV_W)) * g.astype(f32)
    return y.astype(z.dtype), v_first


def mla_attention(zq, zkv, k_rope, positions, q_norm, q_b, kv_norm, kv_b):
    B, S, _ = zq.shape
    q = (rms_norm(zq, q_norm) @ q_b).reshape(B, S, MLA_HEADS, QK_NOPE + QK_ROPE)
    kv = (rms_norm(zkv, kv_norm) @ kv_b).reshape(B, S, MLA_HEADS, QK_NOPE + V_HEAD)
    q_nope, q_rope = q[..., :QK_NOPE], q[..., QK_NOPE:]
    k_nope, v = kv[..., :QK_NOPE], kv[..., QK_NOPE:]
    cos, sin = rope_cos_sin(positions)
    q_rope = apply_rope(q_rope, cos[:, :, None], sin[:, :, None])
    k_rope = apply_rope(k_rope, cos, sin)
    nb = S // Q_BLOCK
    to_blocks = lambda t: jnp.moveaxis(t.reshape(B, nb, Q_BLOCK, *t.shape[2:]), 1, 0)
    key_pos = jnp.arange(S)

    def attend_block(args):
        i, qn, qr = args
        s = (jnp.einsum('bqhd,bkhd->bhqk', qn, k_nope)
             + jnp.einsum('bqhd,bkd->bhqk', qr, k_rope)).astype(jnp.float32) * ATTN_SCALE
        q_pos = i * Q_BLOCK + jnp.arange(Q_BLOCK)
        s = jnp.where(key_pos[None, :] <= q_pos[:, None], s, -jnp.inf)
        p = jax.nn.softmax(s, axis=-1).astype(v.dtype)
        return jnp.einsum('bhqk,bkhd->bqhd', p, v)

    o = lax.map(attend_block, (jnp.arange(nb), to_blocks(q_nope), to_blocks(q_rope)))
    return jnp.moveaxis(o, 0, 1).reshape(B, S, MLA_W)


def conformer_conv(z, dw, db, ln_g, ln_b):
    u_a, u_b = jnp.split(z, 2, axis=-1)
    u = u_a * jax.nn.sigmoid(u_b)
    u = jnp.pad(u, ((0, 0), (CONV_K - 1, 0), (0, 0)))
    y = lax.conv_general_dilated(u, dw[:, None, :], window_strides=(1,), padding='VALID',
                                 dimension_numbers=('NWC', 'WIO', 'NWC'),
                                 feature_group_count=CONV_W) + db
    return jax.nn.silu(layer_norm(y, ln_g, ln_b))


def mixer(h, positions, v_first, w_in, rw_mu, rw_w0, rw_w2, rw_a0, rw_a2, rw_g2, rw_kk, rw_ka, rw_rk,
          rw_lnx_g, rw_lnx_b, vres, mla_q_norm, mla_q_b, mla_kv_norm, mla_kv_b,
          conv_dw, conv_db, conv_ln_g, conv_ln_b, w_branch, w_out):
    B, S, D = h.shape
    z = h @ w_in
    z_rw, z_q, z_kv, z_kr, z_conv, z_gate = jnp.split(z, IN_SPLITS, axis=-1)
    z_rw = z_rw + (token_shift(z_rw) - z_rw) * rw_mu
    y_a, v_first = rwkv7_time_mix(z_rw, v_first, rw_w0, rw_w2, rw_a0, rw_a2, rw_g2, rw_kk, rw_ka, rw_rk,
                                  rw_lnx_g, rw_lnx_b, vres)
    y_b = mla_attention(z_q, z_kv, z_kr, positions, mla_q_norm, mla_q_b, mla_kv_norm, mla_kv_b)
    y_c = conformer_conv(z_conv, conv_dw, conv_db, conv_ln_g, conv_ln_b)
    gates = jax.nn.sigmoid(z_gate).reshape(B, S, N_BRANCH, D)
    merged = gates[:, :, 0] * (y_a @ w_branch[0])
    merged = merged + gates[:, :, 1] * (y_b @ w_branch[1])
    merged = merged + gates[:, :, 2] * (y_c @ w_branch[2])
    return merged @ w_out, v_first


def moe_ffn(x2, router_w, router_b, w13, w2):
    T, D = x2.shape
    scores = jax.nn.sigmoid(jnp.matmul(x2.astype(jnp.float32), router_w.astype(jnp.float32)))
    sel = (scores + router_b.astype(jnp.float32)).reshape(T, N_GROUPS, EXP_PER_GROUP)
    group_score = jnp.sum(lax.top_k(sel, 2)[0], axis=-1)
    g_idx = jnp.argmax(group_score, axis=-1).astype(jnp.int32)
    in_group = jnp.take_along_axis(sel, g_idx[:, None, None], axis=1)[:, 0]
    _, local = lax.top_k(in_group, TOP_K)
    e_idx = g_idx[:, None] * EXP_PER_GROUP + local.astype(jnp.int32)
    gate = jnp.take_along_axis(scores, e_idx, axis=1)
    gate = (gate / jnp.sum(gate, axis=-1, keepdims=True)).astype(x2.dtype)
    A = T * TOP_K
    flat_e = e_idx.reshape(A)
    flat_tok = jnp.repeat(jnp.arange(T, dtype=jnp.int32), TOP_K)
    flat_w = gate.reshape(A)
    order = jnp.argsort(flat_e)
    se = flat_e[order]
    counts = jnp.bincount(flat_e, length=N_EXPERTS)
    starts = jnp.cumsum(counts) - counts
    padded = (counts + MOE_BLOCK - 1) // MOE_BLOCK * MOE_BLOCK
    ends = jnp.cumsum(padded)
    pstarts = ends - padded
    dest = pstarts[se] + (jnp.arange(A) - starts[se])
    P = (A + MOE_BLOCK - 1) // MOE_BLOCK * MOE_BLOCK + N_EXPERTS * MOE_BLOCK
    nblk = P // MOE_BLOCK
    tok_pad = jnp.full((P,), T, jnp.int32).at[dest].set(flat_tok[order])
    w_pad = jnp.zeros((P,), x2.dtype).at[dest].set(flat_w[order])
    block_e = jnp.clip(jnp.searchsorted(ends, jnp.arange(nblk) * MOE_BLOCK, side='right'), 0, N_EXPERTS - 1)
    x_ext = jnp.concatenate([x2, jnp.zeros((1, D), x2.dtype)], axis=0)

    def expert_block(args):
        e, tok = args
        xb = x_ext[tok]
        gu = xb @ w13[e]
        return (jax.nn.silu(gu[:, :D_EXPERT]) * gu[:, D_EXPERT:]) @ w2[e]

    ys = lax.map(expert_block, (block_e, tok_pad.reshape(nblk, MOE_BLOCK)))
    ys = ys.reshape(P, D) * w_pad[:, None]
    return jax.ops.segment_sum(ys, tok_pad, num_segments=T + 1)[:T]


def setup_inputs(seed: int = 0) -> dict:
    key = jax.random.key(seed)
    ks = iter(jax.random.split(key, 48))
    f32 = jnp.float32
    L = DEPTH
    nrm = lambda shape, scale: jax.random.normal(next(ks), shape, f32) * scale
    uni = lambda shape, lo, hi: jax.random.uniform(next(ks), shape, f32, lo, hi)
    x = jax.random.normal(next(ks), (BATCH, SEQ, D_MODEL), f32)
    positions = (jnp.arange(SEQ, dtype=jnp.int32)[None, :]
                 + jax.random.randint(next(ks), (BATCH, 1), 0, 1024, jnp.int32))
    return {
        'x': x,
        'positions': positions,
        'ln0_g': 1.0 + nrm((D_MODEL,), 0.02),
        'ln0_b': nrm((D_MODEL,), 0.02),
        'w_in': nrm((L, D_MODEL, C_IN), D_MODEL ** -0.5),
        'rw_mu': uni((L, RW_SHIFT), 0.0, 1.0),
        'rw_w0': uni((L, RWKV_W), -5.0, 0.0),
        'rw_w2': nrm((L, DECAY_LORA, RWKV_W), 0.5 * DECAY_LORA ** -0.5),
        'rw_a0': nrm((L, RWKV_W), 0.1),
        'rw_a2': nrm((L, AAA_LORA, RWKV_W), AAA_LORA ** -0.5),
        'rw_g2': nrm((L, GATE_LORA, RWKV_W), GATE_LORA ** -0.5),
        'rw_kk': 0.85 + nrm((L, RWKV_W), 0.02),
        'rw_ka': 1.0 + nrm((L, RWKV_W), 0.02),
        'rw_rk': nrm((L, RWKV_W), 0.1),
        'rw_lnx_g': 1.0 + nrm((L, RWKV_W), 0.02),
        'rw_lnx_b': nrm((L, RWKV_W), 0.02),
        'rw_v0': 1.0 + nrm((L - 1, RWKV_W), 0.02),
        'rw_v1': nrm((L - 1, RWKV_W, MV_LORA), RWKV_W ** -0.5),
        'rw_v2': nrm((L - 1, MV_LORA, RWKV_W), MV_LORA ** -0.5),
        'mla_q_norm': 1.0 + nrm((L, Q_LORA), 0.02),
        'mla_q_b': nrm((L, Q_LORA, MLA_HEADS * (QK_NOPE + QK_ROPE)), Q_LORA ** -0.5),
        'mla_kv_norm': 1.0 + nrm((L, KV_LORA), 0.02),
        'mla_kv_b': nrm((L, KV_LORA, MLA_HEADS * (QK_NOPE + V_HEAD)), KV_LORA ** -0.5),
        'conv_dw': nrm((L, CONV_K, CONV_W), CONV_K ** -0.5),
        'conv_db': nrm((L, CONV_W), 0.02),
        'conv_ln_g': 1.0 + nrm((L, CONV_W), 0.02),
        'conv_ln_b': nrm((L, CONV_W), 0.02),
        'w_branch': nrm((L, N_BRANCH, BRANCH_W, D_MODEL), DN_BETA * BRANCH_W ** -0.5),
        'w_out': nrm((L, D_MODEL, D_MODEL), DN_BETA * D_MODEL ** -0.5),
        'ln1_g': 1.0 + nrm((L, D_MODEL), 0.02),
        'ln1_b': nrm((L, D_MODEL), 0.02),
        'router_w': nrm((D_MODEL, N_EXPERTS), D_MODEL ** -0.5),
        'router_b': nrm((N_EXPERTS,), 0.01),
        'moe_w13': nrm((L, N_EXPERTS, D_MODEL, 2 * D_EXPERT), D_MODEL ** -0.5),
        'moe_w2': nrm((L, N_EXPERTS, D_EXPERT, D_MODEL), DN_BETA * D_EXPERT ** -0.5),
        'ln2_g': 1.0 + nrm((L, D_MODEL), 0.02),
        'ln2_b': nrm((L, D_MODEL), 0.02),
    }


def reference(x, positions, ln0_g, ln0_b, w_in, rw_mu, rw_w0, rw_w2, rw_a0, rw_a2, rw_g2, rw_kk, rw_ka,
              rw_rk, rw_lnx_g, rw_lnx_b, rw_v0, rw_v1, rw_v2, mla_q_norm, mla_q_b, mla_kv_norm, mla_kv_b,
              conv_dw, conv_db, conv_ln_g, conv_ln_b, w_branch, w_out, ln1_g, ln1_b, router_w, router_b,
              moe_w13, moe_w2, ln2_g, ln2_b):
    B, S, D = x.shape
    h = layer_norm(x, ln0_g, ln0_b)
    v_first = None
    for l in range(DEPTH):
        vres = None if l == 0 else (rw_v0[l - 1], rw_v1[l - 1], rw_v2[l - 1])
        m, v_first = mixer(h, positions, v_first, w_in[l], rw_mu[l], rw_w0[l], rw_w2[l], rw_a0[l], rw_a2[l],
                           rw_g2[l], rw_kk[l], rw_ka[l], rw_rk[l], rw_lnx_g[l], rw_lnx_b[l], vres,
                           mla_q_norm[l], mla_q_b[l], mla_kv_norm[l], mla_kv_b[l],
                           conv_dw[l], conv_db[l], conv_ln_g[l], conv_ln_b[l], w_branch[l], w_out[l])
        h = layer_norm(DN_ALPHA * h + m, ln1_g[l], ln1_b[l])
        f = moe_ffn(h.reshape(B * S, D), router_w, router_b, moe_w13[l], moe_w2[l]).reshape(B, S, D)
        h = layer_norm(DN_ALPHA * h + f, ln2_g[l], ln2_b[l])
    return h
```

```python
import functools

import jax
import jax.numpy as jnp
from jax import lax
from jax.experimental import pallas as pl
from jax.experimental.pallas import tpu as pltpu

F32 = jnp.float32
BF16 = jnp.bfloat16

D_MODEL = 2048
DEPTH = 4
RWKV_HEADS = 16
RWKV_HEAD = 64
RWKV_W = 1024
DECAY_LORA = 64
AAA_LORA = 64
MV_LORA = 32
GATE_LORA = 160
RWKV_GN_EPS = 64e-5
MLA_HEADS = 8
Q_LORA = 512
KV_LORA = 256
QK_NOPE = 128
QK_ROPE = 64
V_HEAD = 128
ROPE_THETA = 10000.0
ATTN_SCALE = (QK_NOPE + QK_ROPE) ** -0.5
CONV_W = 1024
CONV_K = 31
N_BRANCH = 3
N_EXPERTS = 32
N_GROUPS = 4
EXP_PER_GROUP = 8
TOP_K = 2
D_EXPERT = 512
DN_ALPHA = (2 * DEPTH) ** 0.25
LN_EPS = 1e-5
RMS_EPS = 1e-6
RW_SHIFT = 3 * RWKV_W + DECAY_LORA + AAA_LORA + GATE_LORA
LORA_W = DECAY_LORA + AAA_LORA + GATE_LORA

LANES = 128
SUBLANES = 8
VMEM_LIMIT = 48 * 1024 * 1024

LORA_PAD = 384
RW_PAD = 3 * RWKV_W + LORA_PAD
KVR_PAD = 384
QK_PAD = 256

CHUNK = 64
GROUP_W = 256
MOE_TB = 256
NEG_BIG = -1e30


def _cparams(sem):
    return pltpu.CompilerParams(dimension_semantics=sem, vmem_limit_bytes=VMEM_LIMIT)


def _dot(a, b):
    return jnp.dot(a, b, preferred_element_type=F32)


def _dot_nt(a, b):
    return lax.dot_general(a, b, (((1,), (1,)), ((), ())), preferred_element_type=F32)


def _dot_tn(a, b):
    return lax.dot_general(a, b, (((0,), (0,)), ((), ())), preferred_element_type=F32)


def _split2(x):
    hi = x.astype(BF16)
    lo = (x - hi.astype(F32)).astype(BF16)
    return hi, lo


def _split3(x):
    hi = x.astype(BF16)
    r1 = x - hi.astype(F32)
    mid = r1.astype(BF16)
    lo = (r1 - mid.astype(F32)).astype(BF16)
    return hi, mid, lo


def _dot_x2(x, w_bf16):
    hi, lo = _split2(x)
    return _dot(hi, w_bf16) + _dot(lo, w_bf16)


def _sigmoid(x):
    return 1.0 / (1.0 + jnp.exp(-x))


def _mm_kernel(a_ref, b_ref, o_ref):
    o_ref[...] = _dot(a_ref[...], b_ref[...]).astype(o_ref.dtype)


def matmul(a, b, tm, tn, out_dtype=BF16):
    m, k = a.shape
    n = b.shape[1]
    return pl.pallas_call(
        _mm_kernel,
        grid=(n // tn, m // tm),
        in_specs=[pl.BlockSpec((tm, k), lambda j, i: (i, 0)),
                  pl.BlockSpec((k, tn), lambda j, i: (0, j))],
        out_specs=pl.BlockSpec((tm, tn), lambda j, i: (i, j)),
        out_shape=jax.ShapeDtypeStruct((m, n), out_dtype),
        compiler_params=_cparams(("parallel", "parallel")),
        name="matmul",
    )(a, b)


def _ln_rows(x, g, b):
    mu = jnp.mean(x, axis=-1, keepdims=True)
    xc = x - mu
    var = jnp.mean(xc * xc, axis=-1, keepdims=True)
    return xc * lax.rsqrt(var + LN_EPS) * g + b


def _ln0_kernel(x_ref, g_ref, b_ref, h_ref, hb_ref):
    h = _ln_rows(x_ref[...], g_ref[...], b_ref[...])
    h_ref[...] = h
    hb_ref[...] = h.astype(BF16)


def layer_norm0(x2, g, b, tm=256):
    t, d = x2.shape
    row = pl.BlockSpec((tm, d), lambda i: (i, 0))
    vec = pl.BlockSpec((1, d), lambda i: (0, 0))
    return pl.pallas_call(
        _ln0_kernel, grid=(t // tm,), in_specs=[row, vec, vec], out_specs=[row, row],
        out_shape=[jax.ShapeDtypeStruct((t, d), F32), jax.ShapeDtypeStruct((t, d), BF16)],
        compiler_params=_cparams(("parallel",)), name="ln0",
    )(x2, g.reshape(1, d), b.reshape(1, d))


def _rwkv_prep_kernel(has_vres, *refs):
    if has_vres:
        (z_ref, zp_ref, mu_ref, vec_ref, wl_hi_ref, wl_lo_ref, bd_ref, tri_ref, ones_ref,
         vf_ref, v0_ref, v1_ref, v2_ref,
         rt_ref, kap_ref, bt_ref, kt_ref, bh_ref, kh_ref, v_ref, bonus_ref, g_ref, pl8_ref) = refs
    else:
        (z_ref, zp_ref, mu_ref, vec_ref, wl_hi_ref, wl_lo_ref, bd_ref, tri_ref, ones_ref,
         rt_ref, kap_ref, bt_ref, kt_ref, bh_ref, kh_ref, v_ref, bonus_ref, g_ref, pl8_ref) = refs
    tm = z_ref.shape[0]
    first = pl.program_id(1) == 0

    z = z_ref[...].astype(F32)
    prev = zp_ref[SUBLANES - 1:SUBLANES, :].astype(F32)
    prev = jnp.where(first, 0.0, prev)
    row = lax.broadcasted_iota(jnp.int32, z.shape, 0)
    zs = jnp.where(row == 0, prev, pltpu.roll(z, 1, 0))
    z = z + (zs - z) * mu_ref[...]

    r = z[:, 0:RWKV_W]
    k = z[:, RWKV_W:2 * RWKV_W]
    v = z[:, 2 * RWKV_W:3 * RWKV_W]
    zl = z[:, 3 * RWKV_W:RW_PAD]
    col = lax.broadcasted_iota(jnp.int32, zl.shape, 1)
    act = jnp.where(col < DECAY_LORA, jnp.tanh(zl),
                    jnp.where(col < DECAY_LORA + AAA_LORA, zl, _sigmoid(zl)))
    a_hi, a_lo = _split2(act)
    lora = _dot(a_hi, wl_hi_ref[...]) + _dot(a_lo, wl_hi_ref[...]) + _dot(a_hi, wl_lo_ref[...])
    w0 = vec_ref[0:1, :]
    a0 = vec_ref[1:2, :]
    k_k = vec_ref[2:3, :]
    k_a = vec_ref[3:4, :]
    r_k = vec_ref[4:5, :]
    x = -(w0 + lora[:, 0:RWKV_W])
    softplus = jnp.maximum(x, 0.0) + jnp.log(1.0 + jnp.exp(-jnp.abs(x)))
    logd = -jnp.exp(-softplus - 0.5)
    a = _sigmoid(a0 + lora[:, RWKV_W:2 * RWKV_W])
    g = lora[:, 2 * RWKV_W:3 * RWKV_W]

    if has_vres:
        lo_rank = _dot(_dot(v.astype(BF16), v1_ref[...]).astype(BF16), v2_ref[...])
        v = v + (vf_ref[...].astype(F32) - v) * _sigmoid(v0_ref[...] + lo_rank)

    bd = bd_ref[...]

    def seg_sum(t):
        return jnp.concatenate(
            [_dot_x2(t[:, c:c + GROUP_W], bd) for c in range(0, RWKV_W, GROUP_W)], axis=1)

    kk = k * k_k
    kk = kk * lax.rsqrt(jnp.maximum(seg_sum(kk * kk), 1e-24))
    kmod = k * (1.0 + (a - 1.0) * k_a)
    b = kk * a
    bonus = seg_sum(r * kmod * r_k) * v

    l_hi, l_mid, l_lo = _split3(logd)
    tri = tri_ref[...]
    ones = ones_ref[...]
    c = _dot(tri, l_hi) + _dot(tri, l_mid) + _dot(tri, l_lo)
    cl = _dot(ones, l_hi) + _dot(ones, l_mid) + _dot(ones, l_lo)
    e_c = jnp.exp(c)
    e_prev = jnp.exp(c - logd)
    e_inv = jnp.exp(-c)
    e_rest = jnp.exp(cl - c)
    rt_ref[...] = (r * e_c).astype(BF16)
    kap_ref[...] = (kk * e_prev).astype(BF16)
    bt_ref[...] = (b * e_inv).astype(BF16)
    kt_ref[...] = (kmod * e_inv).astype(BF16)
    bh_ref[...] = (b * e_rest).astype(BF16)
    kh_ref[...] = (kmod * e_rest).astype(BF16)
    v_ref[...] = v.astype(BF16)
    bonus_ref[...] = bonus.astype(BF16)
    g_ref[...] = g.astype(BF16)
    e_cl = jnp.exp(cl)
    pl8_ref[...] = jnp.concatenate(
        [e_cl[j * CHUNK:j * CHUNK + SUBLANES] for j in range(tm // CHUNK)], axis=0)


def rwkv_prep(z_rw, batch, seq, mu, vecs, wl_hi, wl_lo, vres, tm=256):
    t = z_rw.shape[0]
    nb = seq // tm
    has_vres = vres is not None
    ii = lax.broadcasted_iota(jnp.int32, (GROUP_W, GROUP_W), 0)
    jj = lax.broadcasted_iota(jnp.int32, (GROUP_W, GROUP_W), 1)
    bd = (ii // RWKV_HEAD == jj // RWKV_HEAD).astype(BF16)
    it = lax.broadcasted_iota(jnp.int32, (tm, tm), 0)
    jt = lax.broadcasted_iota(jnp.int32, (tm, tm), 1)
    same = it // CHUNK == jt // CHUNK
    tri = (same & (jt <= it)).astype(BF16)
    ones = same.astype(BF16)

    zrow = pl.BlockSpec((tm, RW_PAD), lambda b, i: (b * nb + i, 0))
    zprev = pl.BlockSpec(
        (SUBLANES, RW_PAD), lambda b, i: (jnp.maximum((b * nb + i) * (tm // SUBLANES) - 1, 0), 0))
    full = lambda shp: pl.BlockSpec(shp, lambda b, i: (0,) * len(shp))
    wrow = pl.BlockSpec((tm, RWKV_W), lambda b, i: (b * nb + i, 0))
    in_specs = [zrow, zprev, full((1, RW_PAD)), full((SUBLANES, RWKV_W)),
                full((LORA_PAD, 3 * RWKV_W)), full((LORA_PAD, 3 * RWKV_W)),
                full((GROUP_W, GROUP_W)), full((tm, tm)), full((tm, tm))]
    args = [z_rw, z_rw, mu, vecs, wl_hi, wl_lo, bd, tri, ones]
    if has_vres:
        v_first, v0, v1, v2 = vres
        in_specs += [wrow, full((1, RWKV_W)), full((RWKV_W, LANES)), full((LANES, RWKV_W))]
        args += [v_first, v0, v1, v2]
    p8 = tm // CHUNK * SUBLANES
    out_specs = [wrow] * 9 + [pl.BlockSpec((p8, RWKV_W), lambda b, i: (b * nb + i, 0))]
    out_shape = [jax.ShapeDtypeStruct((t, RWKV_W), BF16)] * 9 + [
        jax.ShapeDtypeStruct((t // CHUNK * SUBLANES, RWKV_W), F32)]
    return pl.pallas_call(
        functools.partial(_rwkv_prep_kernel, has_vres),
        grid=(batch, nb), in_specs=in_specs, out_specs=out_specs, out_shape=out_shape,
        compiler_params=_cparams(("parallel", "parallel")), name="rwkv_prep",
    )(*args)


def _rwkv_chunk_kernel(rt_ref, kap_ref, bt_ref, kt_ref, bh_ref, kh_ref, v_ref, pl8_ref,
                       bonus_ref, g_ref, lng_ref, lnb_ref, y_ref, h_ref):
    tb = rt_ref.shape[0]
    w = GROUP_W

    @pl.when(pl.program_id(2) == 0)
    def _():
        h_ref[...] = jnp.zeros_like(h_ref)

    ri = lax.broadcasted_iota(jnp.int32, (w, w), 0)
    ci = lax.broadcasted_iota(jnp.int32, (w, w), 1)
    bd_mask = (ri >> 6) == (ci >> 6)
    eye = ri == ci
    tr = lax.broadcasted_iota(jnp.int32, (CHUNK, w), 0)
    sc = lax.broadcasted_iota(jnp.int32, (CHUNK, w), 1) & (CHUNK - 1)
    strict = sc < tr
    incl = sc <= tr
    seg_mean = jnp.where(bd_mask, 1.0 / RWKV_HEAD, 0.0).astype(BF16)

    def bd(y):
        return jnp.where(bd_mask, jnp.concatenate([y] * (w // CHUNK), axis=0), jnp.zeros((), y.dtype))

    def mm(x, y_bd):
        return _dot(x.astype(BF16), y_bd)

    for c in range(tb // CHUNK):
        rows = slice(c * CHUNK, (c + 1) * CHUNK)
        rt = rt_ref[rows, :]
        kap = kap_ref[rows, :]
        v = v_ref[rows, :]
        lhs = jnp.concatenate([kap, rt], axis=0)
        a_b = _dot_nt(lhs, bd(bt_ref[rows, :]))
        a_k = _dot_nt(lhs, bd(kt_ref[rows, :]))
        a_bb = jnp.where(strict, a_b[:CHUNK], 0.0)
        a_rb = jnp.where(incl, a_b[CHUNK:], 0.0)
        a_bk = jnp.where(strict, a_k[:CHUNK], 0.0)
        a_rk = jnp.where(incl, a_k[CHUNK:], 0.0)

        n_pow = -a_bb
        t_inv = jnp.where(sc == tr, 1.0, 0.0) + n_pow
        n_bd = bd(n_pow.astype(BF16))
        for _ in range(5):
            n_pow = mm(n_pow, n_bd)
            n_bd = bd(n_pow.astype(BF16))
            t_inv = t_inv + mm(t_inv, n_bd)

        av = mm(jnp.concatenate([a_bk, a_rk], axis=0), bd(v))
        t_bf = t_inv.astype(BF16)
        kap_hat = _dot(t_bf, bd(kap))
        w_mat = _dot(t_bf, bd(av[:CHUNK].astype(BF16)))
        a_rb_bf = a_rb.astype(BF16)
        r_hat = rt.astype(F32) - _dot(a_rb_bf, bd(kap_hat.astype(BF16)))
        y_in = av[CHUNK:] - _dot(a_rb_bf, bd(w_mat.astype(BF16)))

        h0 = h_ref[...]
        h_hi, h_lo = _split2(h0)
        r_hat_bf = r_hat.astype(BF16)
        y = y_in + _dot(r_hat_bf, h_hi) + _dot(r_hat_bf, h_lo)

        bh = bh_ref[rows, :]
        m_mat = jnp.where(eye, pl8_ref[c * SUBLANES:c * SUBLANES + 1, :], 0.0) - jnp.where(
            bd_mask, _dot_tn(bh, kap_hat.astype(BF16)), 0.0)
        g_mat = jnp.where(bd_mask, _dot_tn(kh_ref[rows, :], v) - _dot_tn(bh, w_mat.astype(BF16)), 0.0)
        m_hi, m_lo = _split2(m_mat)
        h_ref[...] = g_mat + _dot(m_hi, h_hi) + _dot(m_hi, h_lo) + _dot(m_lo, h_hi)

        mu = _dot_x2(y, seg_mean)
        yc = y - mu
        var = _dot_x2(yc * yc, seg_mean)
        out = yc * lax.rsqrt(var + RWKV_GN_EPS) * lng_ref[...] + lnb_ref[...]
        out = (out + bonus_ref[rows, :].astype(F32)) * g_ref[rows, :].astype(F32)
        y_ref[rows, :] = out.astype(y_ref.dtype)


def rwkv_chunks(prep, batch, seq, lnx_g, lnx_b, tb=256):
    rt, kap, bt, kt, bh, kh, v, bonus, g, pl8 = prep
    t = rt.shape[0]
    nb = seq // tb
    ng = RWKV_W // GROUP_W
    blk = pl.BlockSpec((tb, GROUP_W), lambda b, q, i: (b * nb + i, q))
    p8 = pl.BlockSpec((tb // CHUNK * SUBLANES, GROUP_W), lambda b, q, i: (b * nb + i, q))
    vec = pl.BlockSpec((1, GROUP_W), lambda b, q, i: (0, q))
    return pl.pallas_call(
        _rwkv_chunk_kernel,
        grid=(batch, ng, nb),
        in_specs=[blk] * 7 + [p8, blk, blk, vec, vec],
        out_specs=blk,
        out_shape=jax.ShapeDtypeStruct((t, RWKV_W), BF16),
        scratch_shapes=[pltpu.VMEM((GROUP_W, GROUP_W), F32)],
        compiler_params=_cparams(("parallel", "parallel", "arbitrary")), name="rwkv_chunks",
    )(rt, kap, bt, kt, bh, kh, v, pl8, bonus, g, lnx_g.reshape(1, RWKV_W), lnx_b.reshape(1, RWKV_W))


def _mla_prep_kernel(zq_ref, zkv_ref, cos_ref, sin_ref, qn_ref, kvn_ref, wq_ref, wqr_ref,
                     wk_ref, wv_ref, e_ref, er_ref, q_ref, k_ref, v_ref):
    zq = zq_ref[...].astype(F32)
    qn = (zq * lax.rsqrt(jnp.mean(zq * zq, axis=-1, keepdims=True) + RMS_EPS) * qn_ref[...]).astype(BF16)
    zkv = zkv_ref[:, 0:KV_LORA].astype(F32)
    cn = (zkv * lax.rsqrt(jnp.mean(zkv * zkv, axis=-1, keepdims=True) + RMS_EPS) * kvn_ref[...]).astype(BF16)
    kr = zkv_ref[:, KV_LORA:KVR_PAD]
    cos = cos_ref[...]
    sin = sin_ref[...]
    q_all = _dot(qn, wq_ref[...])
    q_rot = _dot(qn, wqr_ref[...])
    k_all = _dot(cn, wk_ref[...]) + _dot(kr, e_ref[...])
    k_rot = _dot(kr, er_ref[...])
    for h in range(MLA_HEADS):
        cols = slice(h * QK_PAD, (h + 1) * QK_PAD)
        q_ref[:, cols] = ((q_all[:, cols] * cos + q_rot[:, cols] * sin) * ATTN_SCALE).astype(BF16)
        k_ref[:, cols] = (k_all[:, cols] * cos + k_rot[:, cols] * sin).astype(BF16)
    v_ref[...] = _dot(cn, wv_ref[...]).astype(BF16)


def mla_prep(z_q, z_kvr, cos_t, sin_t, qn, kvn, wq, wqr, wk, wv, e_mat, er_mat, tm=256):
    t = z_q.shape[0]
    hw = MLA_HEADS * QK_PAD
    row = lambda w: pl.BlockSpec((tm, w), lambda i: (i, 0))
    full = lambda a: pl.BlockSpec(a.shape, lambda i: (0, 0))
    return pl.pallas_call(
        _mla_prep_kernel, grid=(t // tm,),
        in_specs=[row(Q_LORA), row(KVR_PAD), row(QK_PAD), row(QK_PAD), full(qn), full(kvn),
                  full(wq), full(wqr), full(wk), full(wv), full(e_mat), full(er_mat)],
        out_specs=[row(hw), row(hw), row(MLA_HEADS * V_HEAD)],
        out_shape=[jax.ShapeDtypeStruct((t, hw), BF16), jax.ShapeDtypeStruct((t, hw), BF16),
                   jax.ShapeDtypeStruct((t, MLA_HEADS * V_HEAD), BF16)],
        compiler_params=_cparams(("parallel",)), name="mla_prep",
    )(z_q, z_kvr, cos_t, sin_t, qn, kvn, wq, wqr, wk, wv, e_mat, er_mat)


def _attn_kernel(q_ref, k_ref, v_ref, o_ref):
    tq = q_ref.shape[0]
    i = pl.program_id(2)
    q = q_ref[...]

    def step(j, carry, masked):
        m, l, acc = carry
        start = pl.multiple_of(j * tq, tq)
        k = k_ref[pl.ds(start, tq), :]
        v = v_ref[pl.ds(start, tq), :]
        s = _dot_nt(q, k)
        if masked:
            qi = lax.broadcasted_iota(jnp.int32, s.shape, 0)
            ki = lax.broadcasted_iota(jnp.int32, s.shape, 1)
            s = jnp.where(ki <= qi, s, NEG_BIG)
        m_new = jnp.maximum(m, jnp.max(s, axis=-1, keepdims=True))
        alpha = jnp.exp(m - m_new)
        p = jnp.exp(s - m_new)
        l = alpha * l + jnp.sum(p, axis=-1, keepdims=True)
        acc = alpha * acc + _dot(p.astype(BF16), v)
        return m_new, l, acc

    init = (jnp.full((tq, 1), NEG_BIG, F32), jnp.zeros((tq, 1), F32), jnp.zeros((tq, V_HEAD), F32))
    carry = lax.fori_loop(0, i, lambda j, c: step(j, c, False), init)
    m, l, acc = step(i, carry, True)
    o_ref[...] = (acc / l).astype(o_ref.dtype)


def attention(q_all, k_all, v_all, batch, seq, tq=512):
    t = q_all.shape[0]
    nq = seq // tq
    return pl.pallas_call(
        _attn_kernel,
        grid=(batch, MLA_HEADS, nq),
        in_specs=[pl.BlockSpec((tq, QK_PAD), lambda b, h, i: (b * nq + i, h)),
                  pl.BlockSpec((seq, QK_PAD), lambda b, h, i: (b, h)),
                  pl.BlockSpec((seq, V_HEAD), lambda b, h, i: (b, h))],
        out_specs=pl.BlockSpec((tq, V_HEAD), lambda b, h, i: (b * nq + i, h)),
        out_shape=jax.ShapeDtypeStruct((t, MLA_HEADS * V_HEAD), BF16),
        compiler_params=_cparams(("parallel", "parallel", "arbitrary")), name="attention",
    )(q_all, k_all, v_all)


CONV_HALO = 32


def _conv_kernel(z_ref, zh_ref, dw_ref, db_ref, g_ref, b_ref, o_ref, u_ref):
    tm = z_ref.shape[0]
    first = pl.program_id(1) == 0

    def glu(zz):
        zz = zz.astype(F32)
        return zz[:, :CONV_W] * _sigmoid(zz[:, CONV_W:])

    u_ref[0:CONV_HALO, :] = jnp.where(first, 0.0, glu(zh_ref[...]))
    u_ref[CONV_HALO:, :] = glu(z_ref[...])
    off = CONV_HALO - (CONV_K - 1)
    acc = jnp.zeros((tm, CONV_W), F32) + db_ref[...]
    for j in range(CONV_K):
        acc = acc + dw_ref[j:j + 1, :] * u_ref[off + j:off + j + tm, :]
    y = _ln_rows(acc, g_ref[...], b_ref[...])
    o_ref[...] = (y * _sigmoid(y)).astype(o_ref.dtype)


def conformer_conv(z_conv, batch, seq, dw, db, ln_g, ln_b, tm=256):
    t = z_conv.shape[0]
    nb = seq // tm
    ratio = tm // CONV_HALO
    vec = pl.BlockSpec((1, CONV_W), lambda b, i: (0, 0))
    return pl.pallas_call(
        _conv_kernel, grid=(batch, nb),
        in_specs=[pl.BlockSpec((tm, 2 * CONV_W), lambda b, i: (b * nb + i, 0)),
                  pl.BlockSpec((CONV_HALO, 2 * CONV_W),
                               lambda b, i: (jnp.maximum((b * nb + i) * ratio - 1, 0), 0)),
                  pl.BlockSpec((CONV_HALO, CONV_W), lambda b, i: (0, 0)), vec, vec, vec],
        out_specs=pl.BlockSpec((tm, CONV_W), lambda b, i: (b * nb + i, 0)),
        out_shape=jax.ShapeDtypeStruct((t, CONV_W), BF16),
        scratch_shapes=[pltpu.VMEM((tm + CONV_HALO, CONV_W), F32)],
        compiler_params=_cparams(("parallel", "parallel")), name="conformer_conv",
    )(z_conv, z_conv, dw, db.reshape(1, CONV_W), ln_g.reshape(1, CONV_W), ln_b.reshape(1, CONV_W))


def _merge_kernel(ya_ref, yb_ref, yc_ref, ga_ref, gb_ref, gc_ref, wb_ref, o_ref):
    acc = None
    for i, (y_ref, zg_ref) in enumerate(((ya_ref, ga_ref), (yb_ref, gb_ref), (yc_ref, gc_ref))):
        term = _sigmoid(zg_ref[...].astype(F32)) * _dot(y_ref[...], wb_ref[i])
        acc = term if acc is None else acc + term
    o_ref[...] = acc.astype(o_ref.dtype)


def branch_merge(y_a, y_b, y_c, zg, wb, tm=512, tn=1024):
    t = y_a.shape[0]
    yspec = pl.BlockSpec((tm, RWKV_W), lambda j, i: (i, 0))
    gspec = pl.BlockSpec((tm, tn), lambda j, i: (i, j))
    return pl.pallas_call(
        _merge_kernel, grid=(D_MODEL // tn, t // tm),
        in_specs=[yspec, yspec, yspec, gspec, gspec, gspec,
                  pl.BlockSpec((N_BRANCH, RWKV_W, tn), lambda j, i: (0, 0, j))],
        out_specs=pl.BlockSpec((tm, tn), lambda j, i: (i, j)),
        out_shape=jax.ShapeDtypeStruct((t, D_MODEL), BF16),
        compiler_params=_cparams(("parallel", "parallel")), name="branch_merge",
    )(y_a, y_b, y_c, *zg, wb)


def _out_ln_router_kernel(m_ref, w_ref, h_ref, g_ref, b_ref, rw_hi_ref, rw_lo_ref, ho_ref, hb_ref, sc_ref):
    x = DN_ALPHA * h_ref[...] + _dot(m_ref[...], w_ref[...])
    h = _ln_rows(x, g_ref[...], b_ref[...])
    ho_ref[...] = h
    hb_ref[...] = h.astype(BF16)
    h_hi, h_lo = _split2(h)
    logits = _dot(h_hi, rw_hi_ref[...]) + _dot(h_lo, rw_hi_ref[...]) + _dot(h_hi, rw_lo_ref[...])
    sc_ref[...] = _sigmoid(logits)


def out_ln_router(merged, w_out, h, g, b, rw_hi, rw_lo, tm=256):
    t, d = h.shape
    row = pl.BlockSpec((tm, d), lambda i: (i, 0))
    vec = pl.BlockSpec((1, d), lambda i: (0, 0))
    rws = pl.BlockSpec((d, LANES), lambda i: (0, 0))
    return pl.pallas_call(
        _out_ln_router_kernel, grid=(t // tm,),
        in_specs=[row, pl.BlockSpec((d, d), lambda i: (0, 0)), row, vec, vec, rws, rws],
        out_specs=[row, row, pl.BlockSpec((tm, LANES), lambda i: (i, 0))],
        out_shape=[jax.ShapeDtypeStruct((t, d), F32), jax.ShapeDtypeStruct((t, d), BF16),
                   jax.ShapeDtypeStruct((t, LANES), F32)],
        compiler_params=_cparams(("parallel",)), name="out_ln_router",
    )(merged, w_out, h, g.reshape(1, d), b.reshape(1, d), rw_hi, rw_lo)


def _moe_kernel(be_ref, nv_ref, x_ref, w13_ref, w2_ref, o_ref):
    del be_ref
    i = pl.program_id(0)

    @pl.when(i < nv_ref[0])
    def _():
        gu = _dot(x_ref[...], w13_ref[...])
        gate = gu[:, :D_EXPERT]
        act = gate * _sigmoid(gate) * gu[:, D_EXPERT:]
        o_ref[...] = _dot(act.astype(BF16), w2_ref[...]).astype(o_ref.dtype)

    @pl.when(i >= nv_ref[0])
    def _():
        o_ref[...] = jnp.zeros_like(o_ref)


def moe_experts(x_sorted, block_e, n_valid, w13, w2):
    p, d = x_sorted.shape
    nblk = p // MOE_TB
    grid_spec = pltpu.PrefetchScalarGridSpec(
        num_scalar_prefetch=2, grid=(nblk,),
        in_specs=[pl.BlockSpec((MOE_TB, d), lambda i, be, nv: (i, 0)),
                  pl.BlockSpec((None, d, 2 * D_EXPERT), lambda i, be, nv: (be[i], 0, 0)),
                  pl.BlockSpec((None, D_EXPERT, d), lambda i, be, nv: (be[i], 0, 0))],
        out_specs=pl.BlockSpec((MOE_TB, d), lambda i, be, nv: (i, 0)))
    return pl.pallas_call(
        _moe_kernel, grid_spec=grid_spec,
        out_shape=jax.ShapeDtypeStruct((p, d), BF16),
        compiler_params=_cparams(("arbitrary",)), name="moe_experts",
    )(block_e, n_valid, x_sorted, w13, w2)


def _combine_ln_kernel(final, h_ref, y0_ref, y1_ref, gw_ref, g_ref, b_ref, *outs):
    gw = gw_ref[...]
    f = gw[:, 0:1] * y0_ref[...].astype(F32) + gw[:, 1:2] * y1_ref[...].astype(F32)
    h = _ln_rows(DN_ALPHA * h_ref[...] + f, g_ref[...], b_ref[...])
    outs[0][...] = h
    if not final:
        outs[1][...] = h.astype(BF16)


def combine_ln(h, y0, y1, gate_w, g, b, final, tm=256):
    t, d = h.shape
    row = pl.BlockSpec((tm, d), lambda i: (i, 0))
    vec = pl.BlockSpec((1, d), lambda i: (0, 0))
    out_specs = [row] if final else [row, row]
    out_shape = [jax.ShapeDtypeStruct((t, d), F32)] + ([] if final else [jax.ShapeDtypeStruct((t, d), BF16)])
    return pl.pallas_call(
        functools.partial(_combine_ln_kernel, final), grid=(t // tm,),
        in_specs=[row, row, row, pl.BlockSpec((tm, LANES), lambda i: (i, 0)), vec, vec],
        out_specs=out_specs, out_shape=out_shape,
        compiler_params=_cparams(("parallel",)), name="combine_ln",
    )(h, y0, y1, gate_w, g.reshape(1, d), b.reshape(1, d))


def _route(scores, router_b):
    t = scores.shape[0]
    sel = (scores + router_b).reshape(t, N_GROUPS, EXP_PER_GROUP)
    group_score = jnp.sum(lax.top_k(sel, 2)[0], axis=-1)
    g_idx = jnp.argmax(group_score, axis=-1).astype(jnp.int32)
    in_group = jnp.take_along_axis(sel, g_idx[:, None, None], axis=1)[:, 0]
    _, local = lax.top_k(in_group, TOP_K)
    e_idx = g_idx[:, None] * EXP_PER_GROUP + local.astype(jnp.int32)
    gate = jnp.take_along_axis(scores, e_idx, axis=1)
    gate = gate / jnp.sum(gate, axis=-1, keepdims=True)
    a = t * TOP_K
    flat_e = e_idx.reshape(a)
    onehot = (flat_e[:, None] == jnp.arange(N_EXPERTS, dtype=jnp.int32)[None, :]).astype(jnp.int32)
    csum = jnp.cumsum(onehot, axis=0)
    counts = csum[-1]
    rank = jnp.take_along_axis(csum, flat_e[:, None], axis=1)[:, 0] - 1
    padded = (counts + MOE_TB - 1) // MOE_TB * MOE_TB
    ends = jnp.cumsum(padded)
    pstarts = ends - padded
    dest = pstarts[flat_e] + rank
    p = a + N_EXPERTS * MOE_TB
    nblk = p // MOE_TB
    flat_tok = jnp.repeat(jnp.arange(t, dtype=jnp.int32), TOP_K)
    tok_pad = jnp.zeros((p,), jnp.int32).at[dest].set(flat_tok)
    block_e = jnp.clip(jnp.searchsorted(ends, jnp.arange(nblk, dtype=jnp.int32) * MOE_TB, side='right'),
                       0, N_EXPERTS - 1).astype(jnp.int32)
    n_valid = (ends[-1] // MOE_TB).astype(jnp.int32).reshape(1)
    dest2 = dest.reshape(t, TOP_K)
    gate_w = jnp.pad(gate, ((0, 0), (0, LANES - TOP_K)))
    return tok_pad, block_e, n_valid, dest2[:, 0], dest2[:, 1], gate_w


def _pad_cols(w, n):
    return jnp.pad(w, ((0, 0), (0, n - w.shape[1])))


def _pad_rows(w, n):
    return jnp.pad(w, ((0, n - w.shape[0]), (0, 0)))


def _rope_tables(positions):
    inv = ROPE_THETA ** (-jnp.arange(0, QK_ROPE, 2, dtype=F32) / QK_ROPE)
    ang = positions.astype(F32).reshape(-1)[:, None] * inv
    cos, sin = jnp.cos(ang), jnp.sin(ang)
    t = ang.shape[0]
    ones = jnp.ones((t, QK_NOPE), F32)
    zeros = jnp.zeros((t, QK_PAD - QK_NOPE - QK_ROPE), F32)
    cos_t = jnp.concatenate([ones, cos, cos, zeros], axis=1)
    sin_t = jnp.concatenate([jnp.zeros((t, QK_NOPE), F32), sin, sin, zeros], axis=1)
    return cos_t, sin_t


def _rot_cols(w):
    half = QK_ROPE // 2
    return jnp.concatenate([-w[..., half:], w[..., :half]], axis=-1)


def _mla_weights(q_b, kv_b):
    qb = q_b.reshape(Q_LORA, MLA_HEADS, QK_NOPE + QK_ROPE)
    zpad = jnp.zeros((Q_LORA, MLA_HEADS, QK_PAD - QK_NOPE - QK_ROPE), F32)
    wq = jnp.concatenate([qb, zpad], axis=-1).reshape(Q_LORA, MLA_HEADS * QK_PAD)
    wqr = jnp.concatenate([jnp.zeros((Q_LORA, MLA_HEADS, QK_NOPE), F32), _rot_cols(qb[..., QK_NOPE:]), zpad],
                          axis=-1).reshape(Q_LORA, MLA_HEADS * QK_PAD)
    kvb = kv_b.reshape(KV_LORA, MLA_HEADS, QK_NOPE + V_HEAD)
    wk = jnp.concatenate([kvb[..., :QK_NOPE], jnp.zeros((KV_LORA, MLA_HEADS, QK_PAD - QK_NOPE), F32)],
                         axis=-1).reshape(KV_LORA, MLA_HEADS * QK_PAD)
    wv = kvb[..., QK_NOPE:].reshape(KV_LORA, MLA_HEADS * V_HEAD)
    eye = jnp.eye(QK_ROPE, dtype=F32)
    place = lambda blk: _pad_rows(jnp.tile(jnp.concatenate(
        [jnp.zeros((QK_ROPE, QK_NOPE), F32), blk, jnp.zeros((QK_ROPE, QK_PAD - QK_NOPE - QK_ROPE), F32)],
        axis=1), (1, MLA_HEADS)), KVR_PAD - KV_LORA)
    e_mat = place(eye)
    er_mat = place(_rot_cols(eye))
    return tuple(m.astype(BF16) for m in (wq, wqr, wk, wv, e_mat, er_mat))


def kernel(x, positions, ln0_g, ln0_b, w_in, rw_mu, rw_w0, rw_w2, rw_a0, rw_a2, rw_g2, rw_kk, rw_ka, rw_rk, rw_lnx_g, rw_lnx_b, rw_v0, rw_v1, rw_v2, mla_q_norm, mla_q_b, mla_kv_norm, mla_kv_b, conv_dw, conv_db, conv_ln_g, conv_ln_b, w_branch, w_out, ln1_g, ln1_b, router_w, router_b, moe_w13, moe_w2, ln2_g, ln2_b):
    batch, seq, d = x.shape
    t = batch * seq
    h, hb = layer_norm0(x.reshape(t, d), ln0_g, ln0_b)
    cos_t, sin_t = _rope_tables(positions)
    rw_pad = _pad_cols(router_w, LANES)
    rw_hi = rw_pad.astype(BF16)
    rw_lo = (rw_pad - rw_hi.astype(F32)).astype(BF16)
    o_q = RW_SHIFT
    o_kv = o_q + Q_LORA
    o_conv = o_kv + KV_LORA + QK_ROPE
    o_gate = o_conv + 2 * CONV_W
    v_first = None
    for l in range(DEPTH):
        wi = w_in[l]
        w_rw = _pad_cols(wi[:, :RW_SHIFT], RW_PAD).astype(BF16)
        w_q = wi[:, o_q:o_kv].astype(BF16)
        w_kvr = _pad_cols(wi[:, o_kv:o_conv], KVR_PAD).astype(BF16)
        w_conv = wi[:, o_conv:o_gate].astype(BF16)
        w_gate = wi[:, o_gate:].astype(BF16)

        z_rw = matmul(hb, w_rw, 512, RW_PAD // 3)
        mu = _pad_cols(rw_mu[l].reshape(1, RW_SHIFT), RW_PAD)
        vecs = _pad_rows(jnp.stack([rw_w0[l], rw_a0[l], rw_kk[l], rw_ka[l], rw_rk[l]]), SUBLANES)
        wl = jnp.zeros((LORA_PAD, 3 * RWKV_W), F32)
        wl = wl.at[0:DECAY_LORA, 0:RWKV_W].set(rw_w2[l])
        wl = wl.at[DECAY_LORA:DECAY_LORA + AAA_LORA, RWKV_W:2 * RWKV_W].set(rw_a2[l])
        wl = wl.at[DECAY_LORA + AAA_LORA:LORA_W, 2 * RWKV_W:].set(rw_g2[l])
        wl_hi = wl.astype(BF16)
        wl_lo = (wl - wl_hi.astype(F32)).astype(BF16)
        if l == 0:
            vres = None
        else:
            vres = (v_first, rw_v0[l - 1].reshape(1, RWKV_W),
                    _pad_cols(rw_v1[l - 1], LANES).astype(BF16), _pad_rows(rw_v2[l - 1], LANES).astype(BF16))
        prep = rwkv_prep(z_rw, batch, seq, mu, vecs, wl_hi, wl_lo, vres)
        if l == 0:
            v_first = prep[6]
        y_a = rwkv_chunks(prep, batch, seq, rw_lnx_g[l], rw_lnx_b[l])

        z_q = matmul(hb, w_q, 512, Q_LORA)
        z_kvr = matmul(hb, w_kvr, 512, KVR_PAD)
        mw = _mla_weights(mla_q_b[l], mla_kv_b[l])
        q_all, k_all, v_all = mla_prep(z_q, z_kvr, cos_t, sin_t, mla_q_norm[l].reshape(1, Q_LORA),
                                       mla_kv_norm[l].reshape(1, KV_LORA), *mw)
        y_b = attention(q_all, k_all, v_all, batch, seq)

        z_conv = matmul(hb, w_conv, 512, CONV_W)
        y_c = conformer_conv(z_conv, batch, seq, _pad_rows(conv_dw[l], CONV_HALO), conv_db[l],
                             conv_ln_g[l], conv_ln_b[l])

        z_gate = [matmul(hb, w_gate[:, i * d:(i + 1) * d], 512, 1024) for i in range(N_BRANCH)]
        merged = branch_merge(y_a, y_b, y_c, z_gate, w_branch[l].astype(BF16))
        h, hb, scores = out_ln_router(merged, w_out[l].astype(BF16), h, ln1_g[l], ln1_b[l], rw_hi, rw_lo)

        tok_pad, block_e, n_valid, p0, p1, gate_w = _route(scores[:, :N_EXPERTS], router_b)
        ys = moe_experts(hb[tok_pad], block_e, n_valid, moe_w13[l].astype(BF16), moe_w2[l].astype(BF16))
        final = l == DEPTH - 1
        res = combine_ln(h, ys[p0], ys[p1], gate_w, ln2_g[l], ln2_b[l], final)
        if final:
            h = res[0]
        else:
            h, hb = res
    return h.reshape(batch, seq, d)
```

```python
import functools

import jax
import jax.numpy as jnp
from jax import lax
from jax.experimental import pallas as pl
from jax.experimental.pallas import tpu as pltpu

F32 = jnp.float32
BF16 = jnp.bfloat16

D_MODEL = 2048
DEPTH = 4
RWKV_HEADS = 16
RWKV_HEAD = 64
RWKV_W = 1024
DECAY_LORA = 64
AAA_LORA = 64
MV_LORA = 32
GATE_LORA = 160
RWKV_GN_EPS = 64e-5
MLA_HEADS = 8
Q_LORA = 512
KV_LORA = 256
QK_NOPE = 128
QK_ROPE = 64
V_HEAD = 128
ROPE_THETA = 10000.0
ATTN_SCALE = (QK_NOPE + QK_ROPE) ** -0.5
CONV_W = 1024
CONV_K = 31
N_BRANCH = 3
N_EXPERTS = 32
N_GROUPS = 4
EXP_PER_GROUP = 8
TOP_K = 2
D_EXPERT = 512
DN_ALPHA = (2 * DEPTH) ** 0.25
LN_EPS = 1e-5
RMS_EPS = 1e-6
RW_SHIFT = 3 * RWKV_W + DECAY_LORA + AAA_LORA + GATE_LORA
LORA_W = DECAY_LORA + AAA_LORA + GATE_LORA

LANES = 128
SUBLANES = 8
VMEM_LIMIT = 48 * 1024 * 1024

LORA_PAD = 384
RW_PAD = 3 * RWKV_W + LORA_PAD
KVR_PAD = 384
QK_PAD = 256

BF16_ROWS = 16
ATTN_TILE = 512
ATTN_SPLIT = 1
CHUNK = 64
GROUP_W = 256
MOE_TB = 256
NEG_BIG = -1e30


def _cparams(sem):
    return pltpu.CompilerParams(dimension_semantics=sem, vmem_limit_bytes=VMEM_LIMIT)


def _dot(a, b):
    return jnp.dot(a, b, preferred_element_type=F32)


def _dot_nt(a, b):
    return lax.dot_general(a, b, (((1,), (1,)), ((), ())), preferred_element_type=F32)


def _dot_tn(a, b):
    return lax.dot_general(a, b, (((0,), (0,)), ((), ())), preferred_element_type=F32)


def _split2(x):
    hi = x.astype(BF16)
    lo = (x - hi.astype(F32)).astype(BF16)
    return hi, lo


def _split3(x):
    hi = x.astype(BF16)
    r1 = x - hi.astype(F32)
    mid = r1.astype(BF16)
    lo = (r1 - mid.astype(F32)).astype(BF16)
    return hi, mid, lo


def _dot_x2(x, w_bf16):
    hi, lo = _split2(x)
    return _dot(hi, w_bf16) + _dot(lo, w_bf16)


def _sigmoid(x):
    return 1.0 / (1.0 + jnp.exp(-x))


def _mm_kernel(a_ref, b_ref, o_ref):
    o_ref[...] = _dot(a_ref[...], b_ref[...]).astype(o_ref.dtype)


def matmul(a, b, tm, tn, out_dtype=BF16):
    m, k = a.shape
    n = b.shape[1]
    return pl.pallas_call(
        _mm_kernel,
        grid=(n // tn, m // tm),
        in_specs=[pl.BlockSpec((tm, k), lambda j, i: (i, 0)),
                  pl.BlockSpec((k, tn), lambda j, i: (0, j))],
        out_specs=pl.BlockSpec((tm, tn), lambda j, i: (i, j)),
        out_shape=jax.ShapeDtypeStruct((m, n), out_dtype),
        compiler_params=_cparams(("parallel", "parallel")),
        name="matmul",
    )(a, b)


def _ln_rows(x, g, b):
    mu = jnp.mean(x, axis=-1, keepdims=True)
    xc = x - mu
    var = jnp.mean(xc * xc, axis=-1, keepdims=True)
    return xc * lax.rsqrt(var + LN_EPS) * g + b


def _ln0_kernel(x_ref, g_ref, b_ref, h_ref, hb_ref):
    h = _ln_rows(x_ref[...], g_ref[...], b_ref[...])
    h_ref[...] = h
    hb_ref[...] = h.astype(BF16)


def layer_norm0(x2, g, b, tm=256):
    t, d = x2.shape
    row = pl.BlockSpec((tm, d), lambda i: (i, 0))
    vec = pl.BlockSpec((1, d), lambda i: (0, 0))
    return pl.pallas_call(
        _ln0_kernel, grid=(t // tm,), in_specs=[row, vec, vec], out_specs=[row, row],
        out_shape=[jax.ShapeDtypeStruct((t, d), F32), jax.ShapeDtypeStruct((t, d), BF16)],
        compiler_params=_cparams(("parallel",)), name="ln0",
    )(x2, g.reshape(1, d), b.reshape(1, d))


def _rwkv_prep_kernel(has_vres, *refs):
    if has_vres:
        (z_ref, zp_ref, mu_ref, vec_ref, wl_hi_ref, wl_lo_ref, bd_ref, tri_ref, ones_ref,
         vf_ref, v0_ref, v1_ref, v2_ref,
         rt_ref, kap_ref, bt_ref, kt_ref, bh_ref, kh_ref, v_ref, bonus_ref, g_ref, pl8_ref) = refs
    else:
        (z_ref, zp_ref, mu_ref, vec_ref, wl_hi_ref, wl_lo_ref, bd_ref, tri_ref, ones_ref,
         rt_ref, kap_ref, bt_ref, kt_ref, bh_ref, kh_ref, v_ref, bonus_ref, g_ref, pl8_ref) = refs
    tm = z_ref.shape[0]
    first = pl.program_id(1) == 0

    z = z_ref[...].astype(F32)
    prev = zp_ref[SUBLANES - 1:SUBLANES, :].astype(F32)
    prev = jnp.where(first, 0.0, prev)
    row = lax.broadcasted_iota(jnp.int32, z.shape, 0)
    zs = jnp.where(row == 0, prev, pltpu.roll(z, 1, 0))
    z = z + (zs - z) * mu_ref[...]

    r = z[:, 0:RWKV_W]
    k = z[:, RWKV_W:2 * RWKV_W]
    v = z[:, 2 * RWKV_W:3 * RWKV_W]
    zl = z[:, 3 * RWKV_W:RW_PAD]
    col = lax.broadcasted_iota(jnp.int32, zl.shape, 1)
    act = jnp.where(col < DECAY_LORA, jnp.tanh(zl),
                    jnp.where(col < DECAY_LORA + AAA_LORA, zl, _sigmoid(zl)))
    a_hi, a_lo = _split2(act)
    lora = _dot(a_hi, wl_hi_ref[...]) + _dot(a_lo, wl_hi_ref[...]) + _dot(a_hi, wl_lo_ref[...])
    w0 = vec_ref[0:1, :]
    a0 = vec_ref[1:2, :]
    k_k = vec_ref[2:3, :]
    k_a = vec_ref[3:4, :]
    r_k = vec_ref[4:5, :]
    x = -(w0 + lora[:, 0:RWKV_W])
    softplus = jnp.maximum(x, 0.0) + jnp.log(1.0 + jnp.exp(-jnp.abs(x)))
    logd = -jnp.exp(-softplus - 0.5)
    a = _sigmoid(a0 + lora[:, RWKV_W:2 * RWKV_W])
    g = lora[:, 2 * RWKV_W:3 * RWKV_W]

    if has_vres:
        lo_rank = _dot(_dot(v.astype(BF16), v1_ref[...]).astype(BF16), v2_ref[...])
        v = v + (vf_ref[...].astype(F32) - v) * _sigmoid(v0_ref[...] + lo_rank)

    bd = bd_ref[...]

    def seg_sum(t):
        return jnp.concatenate(
            [_dot_x2(t[:, c:c + GROUP_W], bd) for c in range(0, RWKV_W, GROUP_W)], axis=1)

    kk = k * k_k
    kk = kk * lax.rsqrt(jnp.maximum(seg_sum(kk * kk), 1e-24))
    kmod = k * (1.0 + (a - 1.0) * k_a)
    b = kk * a
    bonus = seg_sum(r * kmod * r_k) * v

    l_hi, l_mid, l_lo = _split3(logd)
    tri = tri_ref[...]
    ones = ones_ref[...]
    c = _dot(tri, l_hi) + _dot(tri, l_mid) + _dot(tri, l_lo)
    cl = _dot(ones, l_hi) + _dot(ones, l_mid) + _dot(ones, l_lo)
    e_c = jnp.exp(c)
    e_prev = jnp.exp(c - logd)
    e_inv = jnp.exp(-c)
    e_rest = jnp.exp(cl - c)
    rt_ref[...] = (r * e_c).astype(BF16)
    kap_ref[...] = (kk * e_prev).astype(BF16)
    bt_ref[...] = (b * e_inv).astype(BF16)
    kt_ref[...] = (kmod * e_inv).astype(BF16)
    bh_ref[...] = (b * e_rest).astype(BF16)
    kh_ref[...] = (kmod * e_rest).astype(BF16)
    v_ref[...] = v.astype(BF16)
    bonus_ref[...] = bonus.astype(BF16)
    g_ref[...] = g.astype(BF16)
    e_cl = jnp.exp(cl)
    pl8_ref[...] = jnp.concatenate(
        [e_cl[j * CHUNK:j * CHUNK + SUBLANES] for j in range(tm // CHUNK)], axis=0)


def rwkv_prep(z_rw, batch, seq, mu, vecs, wl_hi, wl_lo, vres, tm=256):
    t = z_rw.shape[0]
    nb = seq // tm
    has_vres = vres is not None
    ii = lax.broadcasted_iota(jnp.int32, (GROUP_W, GROUP_W), 0)
    jj = lax.broadcasted_iota(jnp.int32, (GROUP_W, GROUP_W), 1)
    bd = (ii // RWKV_HEAD == jj // RWKV_HEAD).astype(BF16)
    it = lax.broadcasted_iota(jnp.int32, (tm, tm), 0)
    jt = lax.broadcasted_iota(jnp.int32, (tm, tm), 1)
    same = it // CHUNK == jt // CHUNK
    tri = (same & (jt <= it)).astype(BF16)
    ones = same.astype(BF16)

    zrow = pl.BlockSpec((tm, RW_PAD), lambda b, i: (b * nb + i, 0))
    zprev = pl.BlockSpec(
        (SUBLANES, RW_PAD), lambda b, i: (jnp.maximum((b * nb + i) * (tm // SUBLANES) - 1, 0), 0))
    full = lambda shp: pl.BlockSpec(shp, lambda b, i: (0,) * len(shp))
    wrow = pl.BlockSpec((tm, RWKV_W), lambda b, i: (b * nb + i, 0))
    in_specs = [zrow, zprev, full((1, RW_PAD)), full((SUBLANES, RWKV_W)),
                full((LORA_PAD, 3 * RWKV_W)), full((LORA_PAD, 3 * RWKV_W)),
                full((GROUP_W, GROUP_W)), full((tm, tm)), full((tm, tm))]
    args = [z_rw, z_rw, mu, vecs, wl_hi, wl_lo, bd, tri, ones]
    if has_vres:
        v_first, v0, v1, v2 = vres
        in_specs += [wrow, full((1, RWKV_W)), full((RWKV_W, LANES)), full((LANES, RWKV_W))]
        args += [v_first, v0, v1, v2]
    p8 = tm // CHUNK * SUBLANES
    out_specs = [wrow] * 9 + [pl.BlockSpec((p8, RWKV_W), lambda b, i: (b * nb + i, 0))]
    out_shape = [jax.ShapeDtypeStruct((t, RWKV_W), BF16)] * 9 + [
        jax.ShapeDtypeStruct((t // CHUNK * SUBLANES, RWKV_W), F32)]
    return pl.pallas_call(
        functools.partial(_rwkv_prep_kernel, has_vres),
        grid=(batch, nb), in_specs=in_specs, out_specs=out_specs, out_shape=out_shape,
        compiler_params=_cparams(("parallel", "parallel")), name="rwkv_prep",
    )(*args)


def _rwkv_chunk_kernel(rt_ref, kap_ref, bt_ref, kt_ref, bh_ref, kh_ref, v_ref, pl8_ref,
                       bonus_ref, g_ref, lng_ref, lnb_ref, y_ref, h_ref):
    tb = rt_ref.shape[0]
    w = GROUP_W

    @pl.when(pl.program_id(2) == 0)
    def _():
        h_ref[...] = jnp.zeros_like(h_ref)

    ri = lax.broadcasted_iota(jnp.int32, (w, w), 0)
    ci = lax.broadcasted_iota(jnp.int32, (w, w), 1)
    bd_mask = (ri >> 6) == (ci >> 6)
    eye = ri == ci
    tr = lax.broadcasted_iota(jnp.int32, (CHUNK, w), 0)
    sc = lax.broadcasted_iota(jnp.int32, (CHUNK, w), 1) & (CHUNK - 1)
    strict = sc < tr
    incl = sc <= tr
    seg_mean = jnp.where(bd_mask, 1.0 / RWKV_HEAD, 0.0).astype(BF16)

    def bd(y):
        return jnp.where(bd_mask, jnp.concatenate([y] * (w // CHUNK), axis=0), jnp.zeros((), y.dtype))

    def mm(x, y_bd):
        return _dot(x.astype(BF16), y_bd)

    cs = range(tb // CHUNK)
    rows = [slice(c * CHUNK, (c + 1) * CHUNK) for c in cs]
    rt = [rt_ref[r, :] for r in rows]
    kap = [kap_ref[r, :] for r in rows]
    v = [v_ref[r, :] for r in rows]
    bh = [bh_ref[r, :] for r in rows]
    lhs = [jnp.concatenate([kap[c], rt[c]], axis=0) for c in cs]
    a_b = [_dot_nt(lhs[c], bd(bt_ref[rows[c], :])) for c in cs]
    a_k = [_dot_nt(lhs[c], bd(kt_ref[rows[c], :])) for c in cs]
    a_rb = [jnp.where(incl, a_b[c][CHUNK:], 0.0).astype(BF16) for c in cs]
    a_kk = [jnp.concatenate([jnp.where(strict, a_k[c][:CHUNK], 0.0),
                             jnp.where(incl, a_k[c][CHUNK:], 0.0)], axis=0) for c in cs]

    n_pow = [-jnp.where(strict, a_b[c][:CHUNK], 0.0) for c in cs]
    t_inv = [jnp.where(sc == tr, 1.0, 0.0) + n_pow[c] for c in cs]
    n_bd = [bd(n_pow[c].astype(BF16)) for c in cs]
    for _ in range(5):
        n_pow = [mm(n_pow[c], n_bd[c]) for c in cs]
        n_bd = [bd(n_pow[c].astype(BF16)) for c in cs]
        t_inv = [t_inv[c] + mm(t_inv[c], n_bd[c]) for c in cs]

    av = [mm(a_kk[c], bd(v[c])) for c in cs]
    t_bf = [t_inv[c].astype(BF16) for c in cs]
    kap_hat = [_dot(t_bf[c], bd(kap[c])).astype(BF16) for c in cs]
    w_mat = [_dot(t_bf[c], bd(av[c][:CHUNK].astype(BF16))).astype(BF16) for c in cs]
    r_hat = [(rt[c].astype(F32) - _dot(a_rb[c], bd(kap_hat[c]))).astype(BF16) for c in cs]
    y_in = [av[c][CHUNK:] - _dot(a_rb[c], bd(w_mat[c])) for c in cs]
    m_mat = [jnp.where(eye, pl8_ref[c * SUBLANES:c * SUBLANES + 1, :], 0.0)
             - jnp.where(bd_mask, _dot_tn(bh[c], kap_hat[c]), 0.0) for c in cs]
    g_mat = [jnp.where(bd_mask, _dot_tn(kh_ref[rows[c], :], v[c]) - _dot_tn(bh[c], w_mat[c]), 0.0)
             for c in cs]
    m_split = [_split2(m_mat[c]) for c in cs]

    h = h_ref[...]
    y = []
    for c in cs:
        h_hi, h_lo = _split2(h)
        m_hi, m_lo = m_split[c]
        y.append(y_in[c] + _dot(r_hat[c], h_hi) + _dot(r_hat[c], h_lo))
        h = g_mat[c] + _dot(m_hi, h_hi) + _dot(m_hi, h_lo) + _dot(m_lo, h_hi)
    h_ref[...] = h

    for c in cs:
        mu = _dot_x2(y[c], seg_mean)
        yc = y[c] - mu
        var = _dot_x2(yc * yc, seg_mean)
        out = yc * lax.rsqrt(var + RWKV_GN_EPS) * lng_ref[...] + lnb_ref[...]
        out = (out + bonus_ref[rows[c], :].astype(F32)) * g_ref[rows[c], :].astype(F32)
        y_ref[rows[c], :] = out.astype(y_ref.dtype)


def rwkv_chunks(prep, batch, seq, lnx_g, lnx_b, tb=512):
    rt, kap, bt, kt, bh, kh, v, bonus, g, pl8 = prep
    t = rt.shape[0]
    nb = seq // tb
    ng = RWKV_W // GROUP_W
    blk = pl.BlockSpec((tb, GROUP_W), lambda b, q, i: (b * nb + i, q))
    p8 = pl.BlockSpec((tb // CHUNK * SUBLANES, GROUP_W), lambda b, q, i: (b * nb + i, q))
    vec = pl.BlockSpec((1, GROUP_W), lambda b, q, i: (0, q))
    return pl.pallas_call(
        _rwkv_chunk_kernel,
        grid=(batch, ng, nb),
        in_specs=[blk] * 7 + [p8, blk, blk, vec, vec],
        out_specs=blk,
        out_shape=jax.ShapeDtypeStruct((t, RWKV_W), BF16),
        scratch_shapes=[pltpu.VMEM((GROUP_W, GROUP_W), F32)],
        compiler_params=_cparams(("parallel", "parallel", "arbitrary")), name="rwkv_chunks",
    )(rt, kap, bt, kt, bh, kh, v, pl8, bonus, g, lnx_g.reshape(1, RWKV_W), lnx_b.reshape(1, RWKV_W))


def _mla_prep_kernel(zq_ref, zkv_ref, cos_ref, sin_ref, qn_ref, kvn_ref, wq_ref, wqr_ref,
                     wk_ref, wv_ref, e_ref, er_ref, q_ref, k_ref, v_ref):
    zq = zq_ref[...].astype(F32)
    qn = (zq * lax.rsqrt(jnp.mean(zq * zq, axis=-1, keepdims=True) + RMS_EPS) * qn_ref[...]).astype(BF16)
    zkv = zkv_ref[:, 0:KV_LORA].astype(F32)
    cn = (zkv * lax.rsqrt(jnp.mean(zkv * zkv, axis=-1, keepdims=True) + RMS_EPS) * kvn_ref[...]).astype(BF16)
    kr = zkv_ref[:, KV_LORA:KVR_PAD]
    cos = cos_ref[...]
    sin = sin_ref[...]
    q_all = _dot(qn, wq_ref[...])
    q_rot = _dot(qn, wqr_ref[...])
    k_all = _dot(cn, wk_ref[...]) + _dot(kr, e_ref[...])
    k_rot = _dot(kr, er_ref[...])
    for h in range(MLA_HEADS):
        cols = slice(h * QK_PAD, (h + 1) * QK_PAD)
        q_ref[:, cols] = ((q_all[:, cols] * cos + q_rot[:, cols] * sin) * ATTN_SCALE).astype(BF16)
        k_ref[:, cols] = (k_all[:, cols] * cos + k_rot[:, cols] * sin).astype(BF16)
    v_ref[...] = _dot_nt(wv_ref[...], cn).astype(BF16)


def mla_prep(z_q, z_kvr, cos_t, sin_t, qn, kvn, wq, wqr, wk, wv_t, e_mat, er_mat, tm=ATTN_TILE):
    t = z_q.shape[0]
    hw = MLA_HEADS * QK_PAD
    vw = MLA_HEADS * V_HEAD
    row = lambda w: pl.BlockSpec((tm, w), lambda i: (i, 0))
    full = lambda a: pl.BlockSpec(a.shape, lambda i: (0, 0))
    return pl.pallas_call(
        _mla_prep_kernel, grid=(t // tm,),
        in_specs=[row(Q_LORA), row(KVR_PAD), row(QK_PAD), row(QK_PAD), full(qn), full(kvn),
                  full(wq), full(wqr), full(wk), full(wv_t), full(e_mat), full(er_mat)],
        out_specs=[row(hw), row(hw), pl.BlockSpec((None, vw, tm), lambda i: (i, 0, 0))],
        out_shape=[jax.ShapeDtypeStruct((t, hw), BF16), jax.ShapeDtypeStruct((t, hw), BF16),
                   jax.ShapeDtypeStruct((t // tm, vw, tm), BF16)],
        compiler_params=_cparams(("parallel",)), name="mla_prep",
    )(z_q, z_kvr, cos_t, sin_t, qn, kvn, wq, wqr, wk, wv_t, e_mat, er_mat)


def _attn_kernel(q_ref, k_ref, vt_ref, o_ref):
    tq = q_ref.shape[0]
    tk = tq
    nh = ATTN_SPLIT
    hw = tq // nh
    i = pl.program_id(2)
    qs = [q_ref[s * hw:(s + 1) * hw, :] for s in range(nh)]
    ones = jnp.ones((BF16_ROWS, tk), BF16)

    def step(j, carry, masked):
        ms, accs = carry
        start = pl.multiple_of(j * tk, tk)
        k = k_ref[pl.ds(start, tk), :]
        vt = jnp.concatenate([vt_ref[j], ones], axis=0)
        sts = [_dot_nt(k, q) for q in qs]
        if masked:
            ki = lax.broadcasted_iota(jnp.int32, (tk, hw), 0)
            qi = lax.broadcasted_iota(jnp.int32, (tk, hw), 1)
            sts = [jnp.where(ki <= qi + s * hw, st, NEG_BIG) for s, st in enumerate(sts)]
        m_new = [jnp.maximum(m, jnp.max(st, axis=0, keepdims=True)) for m, st in zip(ms, sts)]
        alpha = [jnp.exp(m - mn) for m, mn in zip(ms, m_new)]
        ps = [jnp.exp(st - mn).astype(BF16) for st, mn in zip(sts, m_new)]
        accs = tuple(a * acc + _dot(vt, p) for a, acc, p in zip(alpha, accs, ps))
        return tuple(m_new), accs

    init = (tuple(jnp.full((1, hw), NEG_BIG, F32) for _ in range(nh)),
            tuple(jnp.zeros((V_HEAD + BF16_ROWS, hw), F32) for _ in range(nh)))
    carry = lax.fori_loop(0, i, lambda j, c: step(j, c, False), init)
    _, accs = step(i, carry, True)
    for s, acc in enumerate(accs):
        out = acc[:V_HEAD] / acc[V_HEAD:V_HEAD + 1]
        o_ref[s * hw:(s + 1) * hw, :] = out.T.astype(o_ref.dtype)


def attention(q_all, k_all, vt_all, batch, seq, tq=ATTN_TILE):
    t = q_all.shape[0]
    nq = seq // tq
    return pl.pallas_call(
        _attn_kernel,
        grid=(batch, MLA_HEADS, nq),
        in_specs=[pl.BlockSpec((tq, QK_PAD), lambda b, h, i: (b * nq + i, h)),
                  pl.BlockSpec((seq, QK_PAD), lambda b, h, i: (b, h)),
                  pl.BlockSpec((nq, V_HEAD, tq), lambda b, h, i: (b, h, 0))],
        out_specs=pl.BlockSpec((tq, V_HEAD), lambda b, h, i: (b * nq + i, h)),
        out_shape=jax.ShapeDtypeStruct((t, MLA_HEADS * V_HEAD), BF16),
        compiler_params=_cparams(("parallel", "parallel", "arbitrary")), name="attention",
    )(q_all, k_all, vt_all)


CONV_HALO = 32


def _conv_kernel(z_ref, zh_ref, dw_ref, db_ref, g_ref, b_ref, o_ref, u_ref):
    tm = z_ref.shape[0]
    first = pl.program_id(1) == 0

    def glu(zz):
        zz = zz.astype(F32)
        return zz[:, :CONV_W] * _sigmoid(zz[:, CONV_W:])

    u_ref[0:CONV_HALO, :] = jnp.where(first, 0.0, glu(zh_ref[...]))
    u_ref[CONV_HALO:, :] = glu(z_ref[...])
    off = CONV_HALO - (CONV_K - 1)
    acc = jnp.zeros((tm, CONV_W), F32) + db_ref[...]
    for j in range(CONV_K):
        acc = acc + dw_ref[j:j + 1, :] * u_ref[off + j:off + j + tm, :]
    y = _ln_rows(acc, g_ref[...], b_ref[...])
    o_ref[...] = (y * _sigmoid(y)).astype(o_ref.dtype)


def conformer_conv(z_conv, batch, seq, dw, db, ln_g, ln_b, tm=256):
    t = z_conv.shape[0]
    nb = seq // tm
    ratio = tm // CONV_HALO
    vec = pl.BlockSpec((1, CONV_W), lambda b, i: (0, 0))
    return pl.pallas_call(
        _conv_kernel, grid=(batch, nb),
        in_specs=[pl.BlockSpec((tm, 2 * CONV_W), lambda b, i: (b * nb + i, 0)),
                  pl.BlockSpec((CONV_HALO, 2 * CONV_W),
                               lambda b, i: (jnp.maximum((b * nb + i) * ratio - 1, 0), 0)),
                  pl.BlockSpec((CONV_HALO, CONV_W), lambda b, i: (0, 0)), vec, vec, vec],
        out_specs=pl.BlockSpec((tm, CONV_W), lambda b, i: (b * nb + i, 0)),
        out_shape=jax.ShapeDtypeStruct((t, CONV_W), BF16),
        scratch_shapes=[pltpu.VMEM((tm + CONV_HALO, CONV_W), F32)],
        compiler_params=_cparams(("parallel", "parallel")), name="conformer_conv",
    )(z_conv, z_conv, dw, db.reshape(1, CONV_W), ln_g.reshape(1, CONV_W), ln_b.reshape(1, CONV_W))


def _merge_kernel(ya_ref, yb_ref, yc_ref, ga_ref, gb_ref, gc_ref, wb_ref, o_ref):
    acc = None
    for i, (y_ref, zg_ref) in enumerate(((ya_ref, ga_ref), (yb_ref, gb_ref), (yc_ref, gc_ref))):
        term = _sigmoid(zg_ref[...].astype(F32)) * _dot(y_ref[...], wb_ref[i])
        acc = term if acc is None else acc + term
    o_ref[...] = acc.astype(o_ref.dtype)


def branch_merge(y_a, y_b, y_c, zg, wb, tm=512, tn=1024):
    t = y_a.shape[0]
    yspec = pl.BlockSpec((tm, RWKV_W), lambda j, i: (i, 0))
    gspec = pl.BlockSpec((tm, tn), lambda j, i: (i, j))
    return pl.pallas_call(
        _merge_kernel, grid=(D_MODEL // tn, t // tm),
        in_specs=[yspec, yspec, yspec, gspec, gspec, gspec,
                  pl.BlockSpec((N_BRANCH, RWKV_W, tn), lambda j, i: (0, 0, j))],
        out_specs=pl.BlockSpec((tm, tn), lambda j, i: (i, j)),
        out_shape=jax.ShapeDtypeStruct((t, D_MODEL), BF16),
        compiler_params=_cparams(("parallel", "parallel")), name="branch_merge",
    )(y_a, y_b, y_c, *zg, wb)


def _out_ln_router_kernel(m_ref, w_ref, h_ref, g_ref, b_ref, rw_hi_ref, rw_lo_ref, rb_ref,
                          ho_ref, hb_ref, ei_ref, gw_ref):
    x = DN_ALPHA * h_ref[...] + _dot(m_ref[...], w_ref[...])
    h = _ln_rows(x, g_ref[...], b_ref[...])
    ho_ref[...] = h
    hb_ref[...] = h.astype(BF16)
    h_hi, h_lo = _split2(h)
    rw_hi = rw_hi_ref[...]
    logits = _dot_nt(rw_hi, h_hi) + _dot_nt(rw_hi, h_lo) + _dot_nt(rw_lo_ref[...], h_hi)
    scores = _sigmoid(logits)
    sel = scores + rb_ref[...]
    tm = sel.shape[1]
    idx = lax.broadcasted_iota(jnp.int32, (EXP_PER_GROUP, tm), 0)

    def first_max(x):
        mx = jnp.max(x, axis=0, keepdims=True)
        return mx, jnp.min(jnp.where(x == mx, idx, EXP_PER_GROUP), axis=0, keepdims=True)

    best = None
    for grp in range(N_GROUPS):
        rows = slice(grp * EXP_PER_GROUP, (grp + 1) * EXP_PER_GROUP)
        x = sel[rows]
        sc_g = scores[rows]
        m1, i1 = first_max(x)
        m2, i2 = first_max(jnp.where(idx == i1, -jnp.inf, x))
        s1 = jnp.sum(jnp.where(idx == i1, sc_g, 0.0), axis=0, keepdims=True)
        s2 = jnp.sum(jnp.where(idx == i2, sc_g, 0.0), axis=0, keepdims=True)
        cand = (m1 + m2, i1 + grp * EXP_PER_GROUP, i2 + grp * EXP_PER_GROUP, s1, s2)
        if best is None:
            best = cand
        else:
            take = cand[0] > best[0]
            best = tuple(jnp.where(take, cn, bs) for cn, bs in zip(cand, best))
    _, e1, e2, s1, s2 = best
    row = lax.broadcasted_iota(jnp.int32, (SUBLANES, tm), 0)
    ei_ref[...] = jnp.where(row == 0, e1, jnp.where(row == 1, e2, 0))
    inv = 1.0 / (s1 + s2)
    gw_ref[...] = jnp.where(row == 0, s1 * inv, jnp.where(row == 1, s2 * inv, 0.0))


def out_ln_router(merged, w_out, h, g, b, rw_hi, rw_lo, rb, tm=256):
    t, d = h.shape
    row = pl.BlockSpec((tm, d), lambda i: (i, 0))
    vec = pl.BlockSpec((1, d), lambda i: (0, 0))
    rws = pl.BlockSpec((N_EXPERTS, d), lambda i: (0, 0))
    tok = pl.BlockSpec((SUBLANES, tm), lambda i: (0, i))
    return pl.pallas_call(
        _out_ln_router_kernel, grid=(t // tm,),
        in_specs=[row, pl.BlockSpec((d, d), lambda i: (0, 0)), row, vec, vec, rws, rws,
                  pl.BlockSpec((N_EXPERTS, tm), lambda i: (0, 0))],
        out_specs=[row, row, tok, tok],
        out_shape=[jax.ShapeDtypeStruct((t, d), F32), jax.ShapeDtypeStruct((t, d), BF16),
                   jax.ShapeDtypeStruct((SUBLANES, t), jnp.int32), jax.ShapeDtypeStruct((SUBLANES, t), F32)],
        compiler_params=_cparams(("parallel",)), name="out_ln_router",
    )(merged, w_out, h, g.reshape(1, d), b.reshape(1, d), rw_hi, rw_lo,
      jnp.broadcast_to(rb.reshape(N_EXPERTS, 1), (N_EXPERTS, tm)))


def _moe_kernel(be_ref, nv_ref, x_ref, w13_ref, w2_ref, o_ref, w13b_ref, w2b_ref):
    i = pl.program_id(0)

    @pl.when((i == 0) | (be_ref[i] != be_ref[jnp.maximum(i - 1, 0)]))
    def _():
        w13b_ref[...] = w13_ref[...].astype(BF16)
        w2b_ref[...] = w2_ref[...].astype(BF16)

    @pl.when(i < nv_ref[0])
    def _():
        gu = _dot(x_ref[...], w13b_ref[...])
        gate = gu[:, :D_EXPERT]
        act = gate * _sigmoid(gate) * gu[:, D_EXPERT:]
        o_ref[...] = _dot(act.astype(BF16), w2b_ref[...]).astype(o_ref.dtype)

    @pl.when(i >= nv_ref[0])
    def _():
        o_ref[...] = jnp.zeros_like(o_ref)


def moe_experts(x_sorted, block_e, n_valid, w13, w2):
    p, d = x_sorted.shape
    nblk = p // MOE_TB
    grid_spec = pltpu.PrefetchScalarGridSpec(
        num_scalar_prefetch=2, grid=(nblk,),
        in_specs=[pl.BlockSpec((MOE_TB, d), lambda i, be, nv: (i, 0)),
                  pl.BlockSpec((None, d, 2 * D_EXPERT), lambda i, be, nv: (be[i], 0, 0)),
                  pl.BlockSpec((None, D_EXPERT, d), lambda i, be, nv: (be[i], 0, 0))],
        out_specs=pl.BlockSpec((MOE_TB, d), lambda i, be, nv: (i, 0)),
        scratch_shapes=[pltpu.VMEM((d, 2 * D_EXPERT), BF16), pltpu.VMEM((D_EXPERT, d), BF16)])
    return pl.pallas_call(
        _moe_kernel, grid_spec=grid_spec,
        out_shape=jax.ShapeDtypeStruct((p, d), BF16),
        compiler_params=_cparams(("arbitrary",)), name="moe_experts",
    )(block_e, n_valid, x_sorted, w13, w2)


def _combine_ln_kernel(final, h_ref, y0_ref, y1_ref, gw_ref, g_ref, b_ref, *outs):
    gw = gw_ref[...]
    f = gw[:, 0:1] * y0_ref[...].astype(F32) + gw[:, 1:2] * y1_ref[...].astype(F32)
    h = _ln_rows(DN_ALPHA * h_ref[...] + f, g_ref[...], b_ref[...])
    outs[0][...] = h
    if not final:
        outs[1][...] = h.astype(BF16)


def combine_ln(h, y0, y1, gate_w, g, b, final, tm=256):
    t, d = h.shape
    row = pl.BlockSpec((tm, d), lambda i: (i, 0))
    vec = pl.BlockSpec((1, d), lambda i: (0, 0))
    out_specs = [row] if final else [row, row]
    out_shape = [jax.ShapeDtypeStruct((t, d), F32)] + ([] if final else [jax.ShapeDtypeStruct((t, d), BF16)])
    return pl.pallas_call(
        functools.partial(_combine_ln_kernel, final), grid=(t // tm,),
        in_specs=[row, row, row, pl.BlockSpec((tm, LANES), lambda i: (i, 0)), vec, vec],
        out_specs=out_specs, out_shape=out_shape,
        compiler_params=_cparams(("parallel",)), name="combine_ln",
    )(h, y0, y1, gate_w, g.reshape(1, d), b.reshape(1, d))


def _route(e_rows, gate_rows):
    e_idx = e_rows[:TOP_K].T
    gate = gate_rows[:TOP_K].T
    t = e_idx.shape[0]
    a = t * TOP_K
    flat_e = e_idx.reshape(a)
    onehot = (flat_e[:, None] == jnp.arange(N_EXPERTS, dtype=jnp.int32)[None, :]).astype(jnp.int32)
    csum = jnp.cumsum(onehot, axis=0)
    counts = csum[-1]
    rank = jnp.take_along_axis(csum, flat_e[:, None], axis=1)[:, 0] - 1
    padded = (counts + MOE_TB - 1) // MOE_TB * MOE_TB
    ends = jnp.cumsum(padded)
    pstarts = ends - padded
    dest = pstarts[flat_e] + rank
    p = a + N_EXPERTS * MOE_TB
    nblk = p // MOE_TB
    flat_tok = jnp.repeat(jnp.arange(t, dtype=jnp.int32), TOP_K)
    tok_pad = jnp.zeros((p,), jnp.int32).at[dest].set(flat_tok)
    block_e = jnp.clip(jnp.searchsorted(ends, jnp.arange(nblk, dtype=jnp.int32) * MOE_TB, side='right'),
                       0, N_EXPERTS - 1).astype(jnp.int32)
    n_valid = (ends[-1] // MOE_TB).astype(jnp.int32).reshape(1)
    dest2 = dest.reshape(t, TOP_K)
    gate_w = jnp.pad(gate, ((0, 0), (0, LANES - TOP_K)))
    return tok_pad, block_e, n_valid, dest2[:, 0], dest2[:, 1], gate_w


def _pad_cols(w, n):
    return jnp.pad(w, ((0, 0), (0, n - w.shape[1])))


def _pad_rows(w, n):
    return jnp.pad(w, ((0, n - w.shape[0]), (0, 0)))


def _rope_tables(positions):
    inv = ROPE_THETA ** (-jnp.arange(0, QK_ROPE, 2, dtype=F32) / QK_ROPE)
    ang = positions.astype(F32).reshape(-1)[:, None] * inv
    cos, sin = jnp.cos(ang), jnp.sin(ang)
    t = ang.shape[0]
    ones = jnp.ones((t, QK_NOPE), F32)
    zeros = jnp.zeros((t, QK_PAD - QK_NOPE - QK_ROPE), F32)
    cos_t = jnp.concatenate([ones, cos, cos, zeros], axis=1)
    sin_t = jnp.concatenate([jnp.zeros((t, QK_NOPE), F32), sin, sin, zeros], axis=1)
    return cos_t, sin_t


def _rot_cols(w):
    half = QK_ROPE // 2
    return jnp.concatenate([-w[..., half:], w[..., :half]], axis=-1)


def _mla_weights(q_b, kv_b):
    qb = q_b.reshape(Q_LORA, MLA_HEADS, QK_NOPE + QK_ROPE)
    zpad = jnp.zeros((Q_LORA, MLA_HEADS, QK_PAD - QK_NOPE - QK_ROPE), F32)
    wq = jnp.concatenate([qb, zpad], axis=-1).reshape(Q_LORA, MLA_HEADS * QK_PAD)
    wqr = jnp.concatenate([jnp.zeros((Q_LORA, MLA_HEADS, QK_NOPE), F32), _rot_cols(qb[..., QK_NOPE:]), zpad],
                          axis=-1).reshape(Q_LORA, MLA_HEADS * QK_PAD)
    kvb = kv_b.reshape(KV_LORA, MLA_HEADS, QK_NOPE + V_HEAD)
    wk = jnp.concatenate([kvb[..., :QK_NOPE], jnp.zeros((KV_LORA, MLA_HEADS, QK_PAD - QK_NOPE), F32)],
                         axis=-1).reshape(KV_LORA, MLA_HEADS * QK_PAD)
    wv = kvb[..., QK_NOPE:].reshape(KV_LORA, MLA_HEADS * V_HEAD).T
    eye = jnp.eye(QK_ROPE, dtype=F32)
    place = lambda blk: _pad_rows(jnp.tile(jnp.concatenate(
        [jnp.zeros((QK_ROPE, QK_NOPE), F32), blk, jnp.zeros((QK_ROPE, QK_PAD - QK_NOPE - QK_ROPE), F32)],
        axis=1), (1, MLA_HEADS)), KVR_PAD - KV_LORA)
    e_mat = place(eye)
    er_mat = place(_rot_cols(eye))
    return tuple(m.astype(BF16) for m in (wq, wqr, wk, wv, e_mat, er_mat))


def kernel(x, positions, ln0_g, ln0_b, w_in, rw_mu, rw_w0, rw_w2, rw_a0, rw_a2, rw_g2, rw_kk, rw_ka, rw_rk, rw_lnx_g, rw_lnx_b, rw_v0, rw_v1, rw_v2, mla_q_norm, mla_q_b, mla_kv_norm, mla_kv_b, conv_dw, conv_db, conv_ln_g, conv_ln_b, w_branch, w_out, ln1_g, ln1_b, router_w, router_b, moe_w13, moe_w2, ln2_g, ln2_b):
    batch, seq, d = x.shape
    t = batch * seq
    h, hb = layer_norm0(x.reshape(t, d), ln0_g, ln0_b)
    cos_t, sin_t = _rope_tables(positions)
    rw_t = router_w.T
    rw_hi = rw_t.astype(BF16)
    rw_lo = (rw_t - rw_hi.astype(F32)).astype(BF16)
    o_q = RW_SHIFT
    o_kv = o_q + Q_LORA
    o_conv = o_kv + KV_LORA + QK_ROPE
    o_gate = o_conv + 2 * CONV_W
    v_first = None
    for l in range(DEPTH):
        wi = w_in[l]
        w_rw = _pad_cols(wi[:, :RW_SHIFT], RW_PAD).astype(BF16)
        w_q = wi[:, o_q:o_kv].astype(BF16)
        w_kvr = _pad_cols(wi[:, o_kv:o_conv], KVR_PAD).astype(BF16)
        w_conv = wi[:, o_conv:o_gate].astype(BF16)
        w_gate = wi[:, o_gate:].astype(BF16)

        z_rw = matmul(hb, w_rw, 512, RW_PAD // 3)
        mu = _pad_cols(rw_mu[l].reshape(1, RW_SHIFT), RW_PAD)
        vecs = _pad_rows(jnp.stack([rw_w0[l], rw_a0[l], rw_kk[l], rw_ka[l], rw_rk[l]]), SUBLANES)
        wl = jnp.zeros((LORA_PAD, 3 * RWKV_W), F32)
        wl = wl.at[0:DECAY_LORA, 0:RWKV_W].set(rw_w2[l])
        wl = wl.at[DECAY_LORA:DECAY_LORA + AAA_LORA, RWKV_W:2 * RWKV_W].set(rw_a2[l])
        wl = wl.at[DECAY_LORA + AAA_LORA:LORA_W, 2 * RWKV_W:].set(rw_g2[l])
        wl_hi = wl.astype(BF16)
        wl_lo = (wl - wl_hi.astype(F32)).astype(BF16)
        if l == 0:
            vres = None
        else:
            vres = (v_first, rw_v0[l - 1].reshape(1, RWKV_W),
                    _pad_cols(rw_v1[l - 1], LANES).astype(BF16), _pad_rows(rw_v2[l - 1], LANES).astype(BF16))
        prep = rwkv_prep(z_rw, batch, seq, mu, vecs, wl_hi, wl_lo, vres)
        if l == 0:
            v_first = prep[6]
        y_a = rwkv_chunks(prep, batch, seq, rw_lnx_g[l], rw_lnx_b[l])

        z_q = matmul(hb, w_q, 512, Q_LORA)
        z_kvr = matmul(hb, w_kvr, 512, KVR_PAD)
        mw = _mla_weights(mla_q_b[l], mla_kv_b[l])
        q_all, k_all, v_all = mla_prep(z_q, z_kvr, cos_t, sin_t, mla_q_norm[l].reshape(1, Q_LORA),
                                       mla_kv_norm[l].reshape(1, KV_LORA), *mw)
        y_b = attention(q_all, k_all, v_all, batch, seq)

        z_conv = matmul(hb, w_conv, 512, CONV_W)
        y_c = conformer_conv(z_conv, batch, seq, _pad_rows(conv_dw[l], CONV_HALO), conv_db[l],
                             conv_ln_g[l], conv_ln_b[l])

        z_gate = [matmul(hb, w_gate[:, i * d:(i + 1) * d], 512, 1024) for i in range(N_BRANCH)]
        merged = branch_merge(y_a, y_b, y_c, z_gate, w_branch[l].astype(BF16))
        h, hb, e_rows, gate_rows = out_ln_router(merged, w_out[l].astype(BF16), h, ln1_g[l], ln1_b[l],
                                                 rw_hi, rw_lo, router_b)

        tok_pad, block_e, n_valid, p0, p1, gate_w = _route(e_rows, gate_rows)
        ys = moe_experts(hb[tok_pad], block_e, n_valid, moe_w13[l], moe_w2[l])
        final = l == DEPTH - 1
        res = combine_ln(h, ys[p0], ys[p1], gate_w, ln2_g[l], ln2_b[l], final)
        if final:
            h = res[0]
        else:
            h, hb = res
    return h.reshape(batch, seq, d)
```

```python
import functools

import jax
import jax.numpy as jnp
from jax import lax
from jax.experimental import pallas as pl
from jax.experimental.pallas import tpu as pltpu

F32 = jnp.float32
BF16 = jnp.bfloat16

D_MODEL = 2048
DEPTH = 4
RWKV_HEADS = 16
RWKV_HEAD = 64
RWKV_W = 1024
DECAY_LORA = 64
AAA_LORA = 64
MV_LORA = 32
GATE_LORA = 160
RWKV_GN_EPS = 64e-5
MLA_HEADS = 8
Q_LORA = 512
KV_LORA = 256
QK_NOPE = 128
QK_ROPE = 64
V_HEAD = 128
ROPE_THETA = 10000.0
ATTN_SCALE = (QK_NOPE + QK_ROPE) ** -0.5
CONV_W = 1024
CONV_K = 31
N_BRANCH = 3
N_EXPERTS = 32
N_GROUPS = 4
EXP_PER_GROUP = 8
TOP_K = 2
D_EXPERT = 512
DN_ALPHA = (2 * DEPTH) ** 0.25
LN_EPS = 1e-5
RMS_EPS = 1e-6
RW_SHIFT = 3 * RWKV_W + DECAY_LORA + AAA_LORA + GATE_LORA
LORA_W = DECAY_LORA + AAA_LORA + GATE_LORA

LANES = 128
SUBLANES = 8
VMEM_LIMIT = 48 * 1024 * 1024

LORA_PAD = 384
RW_PAD = 3 * RWKV_W + LORA_PAD
KVR_PAD = 384
QK_PAD = 256

BF16_ROWS = 16
ATTN_TILE = 512
CHUNK = 64
GROUP_W = 256
MOE_TB = 256
NEG_BIG = -1e30


def _cparams(sem):
    return pltpu.CompilerParams(dimension_semantics=sem, vmem_limit_bytes=VMEM_LIMIT)


def _dot(a, b):
    return jnp.dot(a, b, preferred_element_type=F32)


def _dot_nt(a, b):
    return lax.dot_general(a, b, (((1,), (1,)), ((), ())), preferred_element_type=F32)


def _dot_tn(a, b):
    return lax.dot_general(a, b, (((0,), (0,)), ((), ())), preferred_element_type=F32)


def _split2(x):
    hi = x.astype(BF16)
    lo = (x - hi.astype(F32)).astype(BF16)
    return hi, lo


def _split3(x):
    hi = x.astype(BF16)
    r1 = x - hi.astype(F32)
    mid = r1.astype(BF16)
    lo = (r1 - mid.astype(F32)).astype(BF16)
    return hi, mid, lo


def _dot_x2(x, w_bf16):
    hi, lo = _split2(x)
    return _dot(hi, w_bf16) + _dot(lo, w_bf16)


def _sigmoid(x):
    return 1.0 / (1.0 + jnp.exp(-x))


def _mm_kernel(a_ref, b_ref, o_ref):
    o_ref[...] = _dot(a_ref[...], b_ref[...]).astype(o_ref.dtype)


def matmul(a, b, tm, tn, out_dtype=BF16):
    m, k = a.shape
    n = b.shape[1]
    return pl.pallas_call(
        _mm_kernel,
        grid=(n // tn, m // tm),
        in_specs=[pl.BlockSpec((tm, k), lambda j, i: (i, 0)),
                  pl.BlockSpec((k, tn), lambda j, i: (0, j))],
        out_specs=pl.BlockSpec((tm, tn), lambda j, i: (i, j)),
        out_shape=jax.ShapeDtypeStruct((m, n), out_dtype),
        compiler_params=_cparams(("parallel", "parallel")),
        name="matmul",
    )(a, b)


def _ln_rows(x, g, b):
    mu = jnp.mean(x, axis=-1, keepdims=True)
    xc = x - mu
    var = jnp.mean(xc * xc, axis=-1, keepdims=True)
    return xc * lax.rsqrt(var + LN_EPS) * g + b


def _ln0_kernel(x_ref, g_ref, b_ref, h_ref, hb_ref):
    h = _ln_rows(x_ref[...], g_ref[...], b_ref[...])
    h_ref[...] = h
    hb_ref[...] = h.astype(BF16)


def layer_norm0(x2, g, b, tm=256):
    t, d = x2.shape
    row = pl.BlockSpec((tm, d), lambda i: (i, 0))
    vec = pl.BlockSpec((1, d), lambda i: (0, 0))
    return pl.pallas_call(
        _ln0_kernel, grid=(t // tm,), in_specs=[row, vec, vec], out_specs=[row, row],
        out_shape=[jax.ShapeDtypeStruct((t, d), F32), jax.ShapeDtypeStruct((t, d), BF16)],
        compiler_params=_cparams(("parallel",)), name="ln0",
    )(x2, g.reshape(1, d), b.reshape(1, d))


def _rwkv_prep_kernel(has_vres, *refs):
    if has_vres:
        (z_ref, zp_ref, mu_ref, vec_ref, wl_hi_ref, wl_lo_ref, bd_ref, tri_ref, ones_ref,
         vf_ref, v0_ref, v1_ref, v2_ref,
         rt_ref, kap_ref, bt_ref, kt_ref, bh_ref, kh_ref, v_ref, bonus_ref, g_ref, pl8_ref) = refs
    else:
        (z_ref, zp_ref, mu_ref, vec_ref, wl_hi_ref, wl_lo_ref, bd_ref, tri_ref, ones_ref,
         rt_ref, kap_ref, bt_ref, kt_ref, bh_ref, kh_ref, v_ref, bonus_ref, g_ref, pl8_ref) = refs
    tm = z_ref.shape[0]
    first = pl.program_id(1) == 0

    z = z_ref[...].astype(F32)
    prev = zp_ref[SUBLANES - 1:SUBLANES, :].astype(F32)
    prev = jnp.where(first, 0.0, prev)
    row = lax.broadcasted_iota(jnp.int32, z.shape, 0)
    zs = jnp.where(row == 0, prev, pltpu.roll(z, 1, 0))
    z = z + (zs - z) * mu_ref[...]

    r = z[:, 0:RWKV_W]
    k = z[:, RWKV_W:2 * RWKV_W]
    v = z[:, 2 * RWKV_W:3 * RWKV_W]
    zl = z[:, 3 * RWKV_W:RW_PAD]
    col = lax.broadcasted_iota(jnp.int32, zl.shape, 1)
    act = jnp.where(col < DECAY_LORA, jnp.tanh(zl),
                    jnp.where(col < DECAY_LORA + AAA_LORA, zl, _sigmoid(zl)))
    a_hi, a_lo = _split2(act)
    lora = _dot(a_hi, wl_hi_ref[...]) + _dot(a_lo, wl_hi_ref[...]) + _dot(a_hi, wl_lo_ref[...])
    w0 = vec_ref[0:1, :]
    a0 = vec_ref[1:2, :]
    k_k = vec_ref[2:3, :]
    k_a = vec_ref[3:4, :]
    r_k = vec_ref[4:5, :]
    x = -(w0 + lora[:, 0:RWKV_W])
    softplus = jnp.maximum(x, 0.0) + jnp.log(1.0 + jnp.exp(-jnp.abs(x)))
    logd = -jnp.exp(-softplus - 0.5)
    a = _sigmoid(a0 + lora[:, RWKV_W:2 * RWKV_W])
    g = lora[:, 2 * RWKV_W:3 * RWKV_W]

    if has_vres:
        lo_rank = _dot(_dot(v.astype(BF16), v1_ref[...]).astype(BF16), v2_ref[...])
        v = v + (vf_ref[...].astype(F32) - v) * _sigmoid(v0_ref[...] + lo_rank)

    bd = bd_ref[...]

    def seg_sum(t):
        return jnp.concatenate(
            [_dot_x2(t[:, c:c + GROUP_W], bd) for c in range(0, RWKV_W, GROUP_W)], axis=1)

    kk = k * k_k
    kk = kk * lax.rsqrt(jnp.maximum(seg_sum(kk * kk), 1e-24))
    kmod = k * (1.0 + (a - 1.0) * k_a)
    b = kk * a
    bonus = seg_sum(r * kmod * r_k) * v

    l_hi, l_mid, l_lo = _split3(logd)
    tri = tri_ref[...]
    ones = ones_ref[...]
    c = _dot(tri, l_hi) + _dot(tri, l_mid) + _dot(tri, l_lo)
    cl = _dot(ones, l_hi) + _dot(ones, l_mid) + _dot(ones, l_lo)
    e_c = jnp.exp(c)
    e_prev = jnp.exp(c - logd)
    e_inv = jnp.exp(-c)
    e_rest = jnp.exp(cl - c)
    rt_ref[...] = (r * e_c).astype(BF16)
    kap_ref[...] = (kk * e_prev).astype(BF16)
    bt_ref[...] = (b * e_inv).astype(BF16)
    kt_ref[...] = (kmod * e_inv).astype(BF16)
    bh_ref[...] = (b * e_rest).astype(BF16)
    kh_ref[...] = (kmod * e_rest).astype(BF16)
    v_ref[...] = v.astype(BF16)
    bonus_ref[...] = bonus.astype(BF16)
    g_ref[...] = g.astype(BF16)
    e_cl = jnp.exp(cl)
    pl8_ref[...] = jnp.concatenate(
        [e_cl[j * CHUNK:j * CHUNK + SUBLANES] for j in range(tm // CHUNK)], axis=0)


def rwkv_prep(z_rw, batch, seq, mu, vecs, wl_hi, wl_lo, vres, tm=256):
    t = z_rw.shape[0]
    nb = seq // tm
    has_vres = vres is not None
    ii = lax.broadcasted_iota(jnp.int32, (GROUP_W, GROUP_W), 0)
    jj = lax.broadcasted_iota(jnp.int32, (GROUP_W, GROUP_W), 1)
    bd = (ii // RWKV_HEAD == jj // RWKV_HEAD).astype(BF16)
    it = lax.broadcasted_iota(jnp.int32, (tm, tm), 0)
    jt = lax.broadcasted_iota(jnp.int32, (tm, tm), 1)
    same = it // CHUNK == jt // CHUNK
    tri = (same & (jt <= it)).astype(BF16)
    ones = same.astype(BF16)

    zrow = pl.BlockSpec((tm, RW_PAD), lambda b, i: (b * nb + i, 0))
    zprev = pl.BlockSpec(
        (SUBLANES, RW_PAD), lambda b, i: (jnp.maximum((b * nb + i) * (tm // SUBLANES) - 1, 0), 0))
    full = lambda shp: pl.BlockSpec(shp, lambda b, i: (0,) * len(shp))
    wrow = pl.BlockSpec((tm, RWKV_W), lambda b, i: (b * nb + i, 0))
    in_specs = [zrow, zprev, full((1, RW_PAD)), full((SUBLANES, RWKV_W)),
                full((LORA_PAD, 3 * RWKV_W)), full((LORA_PAD, 3 * RWKV_W)),
                full((GROUP_W, GROUP_W)), full((tm, tm)), full((tm, tm))]
    args = [z_rw, z_rw, mu, vecs, wl_hi, wl_lo, bd, tri, ones]
    if has_vres:
        v_first, v0, v1, v2 = vres
        in_specs += [wrow, full((1, RWKV_W)), full((RWKV_W, LANES)), full((LANES, RWKV_W))]
        args += [v_first, v0, v1, v2]
    p8 = tm // CHUNK * SUBLANES
    out_specs = [wrow] * 9 + [pl.BlockSpec((p8, RWKV_W), lambda b, i: (b * nb + i, 0))]
    out_shape = [jax.ShapeDtypeStruct((t, RWKV_W), BF16)] * 9 + [
        jax.ShapeDtypeStruct((t // CHUNK * SUBLANES, RWKV_W), F32)]
    return pl.pallas_call(
        functools.partial(_rwkv_prep_kernel, has_vres),
        grid=(batch, nb), in_specs=in_specs, out_specs=out_specs, out_shape=out_shape,
        compiler_params=_cparams(("parallel", "parallel")), name="rwkv_prep",
    )(*args)


def _rwkv_chunk_kernel(rt_ref, kap_ref, bt_ref, kt_ref, bh_ref, kh_ref, v_ref, pl8_ref,
                       bonus_ref, g_ref, lng_ref, lnb_ref, y_ref, h_ref):
    tb = rt_ref.shape[0]
    w = GROUP_W

    @pl.when(pl.program_id(2) == 0)
    def _():
        h_ref[...] = jnp.zeros_like(h_ref)

    ri = lax.broadcasted_iota(jnp.int32, (w, w), 0)
    ci = lax.broadcasted_iota(jnp.int32, (w, w), 1)
    bd_mask = (ri >> 6) == (ci >> 6)
    eye = ri == ci
    tr = lax.broadcasted_iota(jnp.int32, (CHUNK, w), 0)
    sc = lax.broadcasted_iota(jnp.int32, (CHUNK, w), 1) & (CHUNK - 1)
    strict = sc < tr
    incl = sc <= tr
    seg_mean = jnp.where(bd_mask, 1.0 / RWKV_HEAD, 0.0).astype(BF16)

    def bd(y):
        return jnp.where(bd_mask, jnp.concatenate([y] * (w // CHUNK), axis=0), jnp.zeros((), y.dtype))

    def mm(x, y_bd):
        return _dot(x.astype(BF16), y_bd)

    cs = range(tb // CHUNK)
    rows = [slice(c * CHUNK, (c + 1) * CHUNK) for c in cs]
    rt = [rt_ref[r, :] for r in rows]
    kap = [kap_ref[r, :] for r in rows]
    v = [v_ref[r, :] for r in rows]
    bh = [bh_ref[r, :] for r in rows]
    lhs = [jnp.concatenate([kap[c], rt[c]], axis=0) for c in cs]
    a_b = [_dot_nt(lhs[c], bd(bt_ref[rows[c], :])) for c in cs]
    a_k = [_dot_nt(lhs[c], bd(kt_ref[rows[c], :])) for c in cs]
    a_rb = [jnp.where(incl, a_b[c][CHUNK:], 0.0).astype(BF16) for c in cs]
    a_kk = [jnp.concatenate([jnp.where(strict, a_k[c][:CHUNK], 0.0),
                             jnp.where(incl, a_k[c][CHUNK:], 0.0)], axis=0) for c in cs]

    n_pow = [-jnp.where(strict, a_b[c][:CHUNK], 0.0) for c in cs]
    t_inv = [jnp.where(sc == tr, 1.0, 0.0) + n_pow[c] for c in cs]
    n_bd = [bd(n_pow[c].astype(BF16)) for c in cs]
    for _ in range(5):
        n_pow = [mm(n_pow[c], n_bd[c]) for c in cs]
        n_bd = [bd(n_pow[c].astype(BF16)) for c in cs]
        t_inv = [t_inv[c] + mm(t_inv[c], n_bd[c]) for c in cs]

    av = [mm(a_kk[c], bd(v[c])) for c in cs]
    t_bf = [t_inv[c].astype(BF16) for c in cs]
    kap_hat = [_dot(t_bf[c], bd(kap[c])).astype(BF16) for c in cs]
    w_mat = [_dot(t_bf[c], bd(av[c][:CHUNK].astype(BF16))).astype(BF16) for c in cs]
    r_hat = [(rt[c].astype(F32) - _dot(a_rb[c], bd(kap_hat[c]))).astype(BF16) for c in cs]
    y_in = [av[c][CHUNK:] - _dot(a_rb[c], bd(w_mat[c])) for c in cs]
    m_mat = [jnp.where(eye, pl8_ref[c * SUBLANES:c * SUBLANES + 1, :], 0.0)
             - jnp.where(bd_mask, _dot_tn(bh[c], kap_hat[c]), 0.0) for c in cs]
    g_mat = [jnp.where(bd_mask, _dot_tn(kh_ref[rows[c], :], v[c]) - _dot_tn(bh[c], w_mat[c]), 0.0)
             for c in cs]
    m_split = [_split2(m_mat[c]) for c in cs]

    h = h_ref[...]
    y = []
    for c in cs:
        h_hi, h_lo = _split2(h)
        m_hi, m_lo = m_split[c]
        y.append(y_in[c] + _dot(r_hat[c], h_hi) + _dot(r_hat[c], h_lo))
        h = g_mat[c] + _dot(m_hi, h_hi) + _dot(m_hi, h_lo) + _dot(m_lo, h_hi)
    h_ref[...] = h

    for c in cs:
        mu = _dot_x2(y[c], seg_mean)
        yc = y[c] - mu
        var = _dot_x2(yc * yc, seg_mean)
        out = yc * lax.rsqrt(var + RWKV_GN_EPS) * lng_ref[...] + lnb_ref[...]
        out = (out + bonus_ref[rows[c], :].astype(F32)) * g_ref[rows[c], :].astype(F32)
        y_ref[rows[c], :] = out.astype(y_ref.dtype)


def rwkv_chunks(prep, batch, seq, lnx_g, lnx_b, tb=512):
    rt, kap, bt, kt, bh, kh, v, bonus, g, pl8 = prep
    t = rt.shape[0]
    nb = seq // tb
    ng = RWKV_W // GROUP_W
    blk = pl.BlockSpec((tb, GROUP_W), lambda b, q, i: (b * nb + i, q))
    p8 = pl.BlockSpec((tb // CHUNK * SUBLANES, GROUP_W), lambda b, q, i: (b * nb + i, q))
    vec = pl.BlockSpec((1, GROUP_W), lambda b, q, i: (0, q))
    return pl.pallas_call(
        _rwkv_chunk_kernel,
        grid=(batch, ng, nb),
        in_specs=[blk] * 7 + [p8, blk, blk, vec, vec],
        out_specs=blk,
        out_shape=jax.ShapeDtypeStruct((t, RWKV_W), BF16),
        scratch_shapes=[pltpu.VMEM((GROUP_W, GROUP_W), F32)],
        compiler_params=_cparams(("parallel", "parallel", "arbitrary")), name="rwkv_chunks",
    )(rt, kap, bt, kt, bh, kh, v, pl8, bonus, g, lnx_g.reshape(1, RWKV_W), lnx_b.reshape(1, RWKV_W))


def _mla_prep_kernel(zq_ref, zkv_ref, cos_ref, sin_ref, qn_ref, kvn_ref, wq_ref, wqr_ref,
                     wk_ref, wv_ref, e_ref, er_ref, q_ref, k_ref, v_ref):
    zq = zq_ref[...].astype(F32)
    qn = (zq * lax.rsqrt(jnp.mean(zq * zq, axis=-1, keepdims=True) + RMS_EPS) * qn_ref[...]).astype(BF16)
    zkv = zkv_ref[:, 0:KV_LORA].astype(F32)
    cn = (zkv * lax.rsqrt(jnp.mean(zkv * zkv, axis=-1, keepdims=True) + RMS_EPS) * kvn_ref[...]).astype(BF16)
    kr = zkv_ref[:, KV_LORA:KVR_PAD]
    cos = cos_ref[...]
    sin = sin_ref[...]
    q_all = _dot(qn, wq_ref[...])
    q_rot = _dot(qn, wqr_ref[...])
    k_all = _dot(cn, wk_ref[...]) + _dot(kr, e_ref[...])
    k_rot = _dot(kr, er_ref[...])
    for h in range(MLA_HEADS):
        cols = slice(h * QK_PAD, (h + 1) * QK_PAD)
        q_ref[:, cols] = ((q_all[:, cols] * cos + q_rot[:, cols] * sin) * ATTN_SCALE).astype(BF16)
        k_ref[:, cols] = (k_all[:, cols] * cos + k_rot[:, cols] * sin).astype(BF16)
    v_ref[...] = _dot_nt(wv_ref[...], cn).astype(BF16)


def mla_prep(z_q, z_kvr, cos_t, sin_t, qn, kvn, wq, wqr, wk, wv_t, e_mat, er_mat, tm=ATTN_TILE):
    t = z_q.shape[0]
    hw = MLA_HEADS * QK_PAD
    vw = MLA_HEADS * V_HEAD
    row = lambda w: pl.BlockSpec((tm, w), lambda i: (i, 0))
    full = lambda a: pl.BlockSpec(a.shape, lambda i: (0, 0))
    return pl.pallas_call(
        _mla_prep_kernel, grid=(t // tm,),
        in_specs=[row(Q_LORA), row(KVR_PAD), row(QK_PAD), row(QK_PAD), full(qn), full(kvn),
                  full(wq), full(wqr), full(wk), full(wv_t), full(e_mat), full(er_mat)],
        out_specs=[row(hw), row(hw), pl.BlockSpec((None, vw, tm), lambda i: (i, 0, 0))],
        out_shape=[jax.ShapeDtypeStruct((t, hw), BF16), jax.ShapeDtypeStruct((t, hw), BF16),
                   jax.ShapeDtypeStruct((t // tm, vw, tm), BF16)],
        compiler_params=_cparams(("parallel",)), name="mla_prep",
    )(z_q, z_kvr, cos_t, sin_t, qn, kvn, wq, wqr, wk, wv_t, e_mat, er_mat)


def _attn_kernel(q_ref, k_ref, vt_ref, o_ref, st0_ref, st1_ref, mx0_ref, mx1_ref, m_ref, acc_ref):
    tq = q_ref.shape[0]
    tk = tq
    i = pl.program_id(2)
    q = q_ref[...]
    ones = jnp.ones((BF16_ROWS, tk), BF16)

    def qk(j, st_ref, mx_ref):
        start = pl.multiple_of(j * tk, tk)
        st = _dot_nt(k_ref[pl.ds(start, tk), :], q)
        st_ref[...] = st
        mx_ref[...] = jnp.max(st, axis=0, keepdims=True)

    def consume(j, st_ref, mx_ref, masked):
        st = st_ref[...]
        if masked:
            ki = lax.broadcasted_iota(jnp.int32, st.shape, 0)
            qi = lax.broadcasted_iota(jnp.int32, st.shape, 1)
            st = jnp.where(ki <= qi, st, NEG_BIG)
            mx = jnp.max(st, axis=0, keepdims=True)
        else:
            mx = mx_ref[...]
        m = m_ref[...]
        m_new = jnp.maximum(m, mx)
        alpha = jnp.exp(m - m_new)
        p = jnp.exp(st - m_new).astype(BF16)
        vt = jnp.concatenate([vt_ref[j], ones], axis=0)
        acc_ref[...] = alpha * acc_ref[...] + _dot(vt, p)
        m_ref[...] = m_new

    m_ref[...] = jnp.full(m_ref.shape, NEG_BIG, F32)
    acc_ref[...] = jnp.zeros(acc_ref.shape, F32)
    qk(0, st0_ref, mx0_ref)

    def pair(jj, carry):
        j = 2 * jj
        qk(j + 1, st1_ref, mx1_ref)
        consume(j, st0_ref, mx0_ref, False)
        qk(j + 2, st0_ref, mx0_ref)
        consume(j + 1, st1_ref, mx1_ref, False)
        return carry

    lax.fori_loop(0, i // 2, pair, 0)

    @pl.when(i % 2 == 0)
    def _():
        consume(i, st0_ref, mx0_ref, True)

    @pl.when(i % 2 == 1)
    def _():
        qk(i, st1_ref, mx1_ref)
        consume(i - 1, st0_ref, mx0_ref, False)
        consume(i, st1_ref, mx1_ref, True)

    acc = acc_ref[...]
    out = acc[:V_HEAD] / acc[V_HEAD:V_HEAD + 1]
    o_ref[...] = out.T.astype(o_ref.dtype)


def attention(q_all, k_all, vt_all, batch, seq, tq=ATTN_TILE):
    t = q_all.shape[0]
    nq = seq // tq
    return pl.pallas_call(
        _attn_kernel,
        grid=(batch, MLA_HEADS, nq),
        in_specs=[pl.BlockSpec((tq, QK_PAD), lambda b, h, i: (b * nq + i, h)),
                  pl.BlockSpec((seq, QK_PAD), lambda b, h, i: (b, h)),
                  pl.BlockSpec((nq, V_HEAD, tq), lambda b, h, i: (b, h, 0))],
        out_specs=pl.BlockSpec((tq, V_HEAD), lambda b, h, i: (b * nq + i, h)),
        out_shape=jax.ShapeDtypeStruct((t, MLA_HEADS * V_HEAD), BF16),
        scratch_shapes=[pltpu.VMEM((tq, tq), F32), pltpu.VMEM((tq, tq), F32),
                        pltpu.VMEM((1, tq), F32), pltpu.VMEM((1, tq), F32),
                        pltpu.VMEM((1, tq), F32), pltpu.VMEM((V_HEAD + BF16_ROWS, tq), F32)],
        compiler_params=_cparams(("parallel", "parallel", "arbitrary")), name="attention",
    )(q_all, k_all, vt_all)


CONV_HALO = 32


def _conv_kernel(z_ref, zh_ref, dw_ref, db_ref, g_ref, b_ref, o_ref, u_ref):
    tm = z_ref.shape[0]
    first = pl.program_id(1) == 0

    def glu(zz):
        zz = zz.astype(F32)
        return zz[:, :CONV_W] * _sigmoid(zz[:, CONV_W:])

    u_ref[0:CONV_HALO, :] = jnp.where(first, 0.0, glu(zh_ref[...]))
    u_ref[CONV_HALO:, :] = glu(z_ref[...])
    off = CONV_HALO - (CONV_K - 1)
    acc = jnp.zeros((tm, CONV_W), F32) + db_ref[...]
    for j in range(CONV_K):
        acc = acc + dw_ref[j:j + 1, :] * u_ref[off + j:off + j + tm, :]
    y = _ln_rows(acc, g_ref[...], b_ref[...])
    o_ref[...] = (y * _sigmoid(y)).astype(o_ref.dtype)


def conformer_conv(z_conv, batch, seq, dw, db, ln_g, ln_b, tm=256):
    t = z_conv.shape[0]
    nb = seq // tm
    ratio = tm // CONV_HALO
    vec = pl.BlockSpec((1, CONV_W), lambda b, i: (0, 0))
    return pl.pallas_call(
        _conv_kernel, grid=(batch, nb),
        in_specs=[pl.BlockSpec((tm, 2 * CONV_W), lambda b, i: (b * nb + i, 0)),
                  pl.BlockSpec((CONV_HALO, 2 * CONV_W),
                               lambda b, i: (jnp.maximum((b * nb + i) * ratio - 1, 0), 0)),
                  pl.BlockSpec((CONV_HALO, CONV_W), lambda b, i: (0, 0)), vec, vec, vec],
        out_specs=pl.BlockSpec((tm, CONV_W), lambda b, i: (b * nb + i, 0)),
        out_shape=jax.ShapeDtypeStruct((t, CONV_W), BF16),
        scratch_shapes=[pltpu.VMEM((tm + CONV_HALO, CONV_W), F32)],
        compiler_params=_cparams(("parallel", "parallel")), name="conformer_conv",
    )(z_conv, z_conv, dw, db.reshape(1, CONV_W), ln_g.reshape(1, CONV_W), ln_b.reshape(1, CONV_W))


def _merge_kernel(ya_ref, yb_ref, yc_ref, ga_ref, gb_ref, gc_ref, wb_ref, o_ref):
    acc = None
    for i, (y_ref, zg_ref) in enumerate(((ya_ref, ga_ref), (yb_ref, gb_ref), (yc_ref, gc_ref))):
        term = _sigmoid(zg_ref[...].astype(F32)) * _dot(y_ref[...], wb_ref[i])
        acc = term if acc is None else acc + term
    o_ref[...] = acc.astype(o_ref.dtype)


def branch_merge(y_a, y_b, y_c, zg, wb, tm=512, tn=1024):
    t = y_a.shape[0]
    yspec = pl.BlockSpec((tm, RWKV_W), lambda j, i: (i, 0))
    gspec = pl.BlockSpec((tm, tn), lambda j, i: (i, j))
    return pl.pallas_call(
        _merge_kernel, grid=(D_MODEL // tn, t // tm),
        in_specs=[yspec, yspec, yspec, gspec, gspec, gspec,
                  pl.BlockSpec((N_BRANCH, RWKV_W, tn), lambda j, i: (0, 0, j))],
        out_specs=pl.BlockSpec((tm, tn), lambda j, i: (i, j)),
        out_shape=jax.ShapeDtypeStruct((t, D_MODEL), BF16),
        compiler_params=_cparams(("parallel", "parallel")), name="branch_merge",
    )(y_a, y_b, y_c, *zg, wb)


def _out_ln_router_kernel(m_ref, w_ref, h_ref, g_ref, b_ref, rw_hi_ref, rw_lo_ref, rb_ref,
                          ho_ref, hb_ref, ei_ref, gw_ref):
    x = DN_ALPHA * h_ref[...] + _dot(m_ref[...], w_ref[...])
    h = _ln_rows(x, g_ref[...], b_ref[...])
    ho_ref[...] = h
    hb_ref[...] = h.astype(BF16)
    h_hi, h_lo = _split2(h)
    rw_hi = rw_hi_ref[...]
    logits = _dot_nt(rw_hi, h_hi) + _dot_nt(rw_hi, h_lo) + _dot_nt(rw_lo_ref[...], h_hi)
    scores = _sigmoid(logits)
    sel = scores + rb_ref[...]
    tm = sel.shape[1]
    idx = lax.broadcasted_iota(jnp.int32, (EXP_PER_GROUP, tm), 0)

    def first_max(x):
        mx = jnp.max(x, axis=0, keepdims=True)
        return mx, jnp.min(jnp.where(x == mx, idx, EXP_PER_GROUP), axis=0, keepdims=True)

    best = None
    for grp in range(N_GROUPS):
        rows = slice(grp * EXP_PER_GROUP, (grp + 1) * EXP_PER_GROUP)
        x = sel[rows]
        sc_g = scores[rows]
        m1, i1 = first_max(x)
        m2, i2 = first_max(jnp.where(idx == i1, -jnp.inf, x))
        s1 = jnp.sum(jnp.where(idx == i1, sc_g, 0.0), axis=0, keepdims=True)
        s2 = jnp.sum(jnp.where(idx == i2, sc_g, 0.0), axis=0, keepdims=True)
        cand = (m1 + m2, i1 + grp * EXP_PER_GROUP, i2 + grp * EXP_PER_GROUP, s1, s2)
        if best is None:
            best = cand
        else:
            take = cand[0] > best[0]
            best = tuple(jnp.where(take, cn, bs) for cn, bs in zip(cand, best))
    _, e1, e2, s1, s2 = best
    row = lax.broadcasted_iota(jnp.int32, (SUBLANES, tm), 0)
    ei_ref[...] = jnp.where(row == 0, e1, jnp.where(row == 1, e2, 0))
    inv = 1.0 / (s1 + s2)
    gw_ref[...] = jnp.where(row == 0, s1 * inv, jnp.where(row == 1, s2 * inv, 0.0))


def out_ln_router(merged, w_out, h, g, b, rw_hi, rw_lo, rb, tm=256):
    t, d = h.shape
    row = pl.BlockSpec((tm, d), lambda i: (i, 0))
    vec = pl.BlockSpec((1, d), lambda i: (0, 0))
    rws = pl.BlockSpec((N_EXPERTS, d), lambda i: (0, 0))
    tok = pl.BlockSpec((SUBLANES, tm), lambda i: (0, i))
    return pl.pallas_call(
        _out_ln_router_kernel, grid=(t // tm,),
        in_specs=[row, pl.BlockSpec((d, d), lambda i: (0, 0)), row, vec, vec, rws, rws,
                  pl.BlockSpec((N_EXPERTS, tm), lambda i: (0, 0))],
        out_specs=[row, row, tok, tok],
        out_shape=[jax.ShapeDtypeStruct((t, d), F32), jax.ShapeDtypeStruct((t, d), BF16),
                   jax.ShapeDtypeStruct((SUBLANES, t), jnp.int32), jax.ShapeDtypeStruct((SUBLANES, t), F32)],
        compiler_params=_cparams(("parallel",)), name="out_ln_router",
    )(merged, w_out, h, g.reshape(1, d), b.reshape(1, d), rw_hi, rw_lo,
      jnp.broadcast_to(rb.reshape(N_EXPERTS, 1), (N_EXPERTS, tm)))


def _moe_kernel(be_ref, nv_ref, x_ref, w13_ref, w2_ref, o_ref, w13b_ref, w2b_ref):
    i = pl.program_id(0)

    @pl.when((i == 0) | (be_ref[i] != be_ref[jnp.maximum(i - 1, 0)]))
    def _():
        w13b_ref[...] = w13_ref[...].astype(BF16)
        w2b_ref[...] = w2_ref[...].astype(BF16)

    @pl.when(i < nv_ref[0])
    def _():
        gu = _dot(x_ref[...], w13b_ref[...])
        gate = gu[:, :D_EXPERT]
        act = gate * _sigmoid(gate) * gu[:, D_EXPERT:]
        o_ref[...] = _dot(act.astype(BF16), w2b_ref[...]).astype(o_ref.dtype)

    @pl.when(i >= nv_ref[0])
    def _():
        o_ref[...] = jnp.zeros_like(o_ref)


def moe_experts(x_sorted, block_e, n_valid, w13, w2, layer):
    p, d = x_sorted.shape
    nblk = p // MOE_TB
    grid_spec = pltpu.PrefetchScalarGridSpec(
        num_scalar_prefetch=2, grid=(nblk,),
        in_specs=[pl.BlockSpec((MOE_TB, d), lambda i, be, nv: (i, 0)),
                  pl.BlockSpec((None, None, d, 2 * D_EXPERT), lambda i, be, nv: (layer, be[i], 0, 0)),
                  pl.BlockSpec((None, None, D_EXPERT, d), lambda i, be, nv: (layer, be[i], 0, 0))],
        out_specs=pl.BlockSpec((MOE_TB, d), lambda i, be, nv: (i, 0)),
        scratch_shapes=[pltpu.VMEM((d, 2 * D_EXPERT), BF16), pltpu.VMEM((D_EXPERT, d), BF16)])
    return pl.pallas_call(
        _moe_kernel, grid_spec=grid_spec,
        out_shape=jax.ShapeDtypeStruct((p, d), BF16),
        compiler_params=_cparams(("arbitrary",)), name="moe_experts",
    )(block_e, n_valid, x_sorted, w13, w2)


def _combine_ln_kernel(final, h_ref, y0_ref, y1_ref, gw_ref, g_ref, b_ref, *outs):
    gw = gw_ref[...]
    f = gw[:, 0:1] * y0_ref[...].astype(F32) + gw[:, 1:2] * y1_ref[...].astype(F32)
    h = _ln_rows(DN_ALPHA * h_ref[...] + f, g_ref[...], b_ref[...])
    outs[0][...] = h
    if not final:
        outs[1][...] = h.astype(BF16)


def combine_ln(h, y0, y1, gate_w, g, b, final, tm=256):
    t, d = h.shape
    row = pl.BlockSpec((tm, d), lambda i: (i, 0))
    vec = pl.BlockSpec((1, d), lambda i: (0, 0))
    out_specs = [row] if final else [row, row]
    out_shape = [jax.ShapeDtypeStruct((t, d), F32)] + ([] if final else [jax.ShapeDtypeStruct((t, d), BF16)])
    return pl.pallas_call(
        functools.partial(_combine_ln_kernel, final), grid=(t // tm,),
        in_specs=[row, row, row, pl.BlockSpec((tm, LANES), lambda i: (i, 0)), vec, vec],
        out_specs=out_specs, out_shape=out_shape,
        compiler_params=_cparams(("parallel",)), name="combine_ln",
    )(h, y0, y1, gate_w, g.reshape(1, d), b.reshape(1, d))


def _route(e_rows, gate_rows):
    e_idx = e_rows[:TOP_K].T
    gate = gate_rows[:TOP_K].T
    t = e_idx.shape[0]
    a = t * TOP_K
    flat_e = e_idx.reshape(a)
    onehot = (flat_e[:, None] == jnp.arange(N_EXPERTS, dtype=jnp.int32)[None, :]).astype(jnp.int32)
    csum = jnp.cumsum(onehot, axis=0)
    counts = csum[-1]
    rank = jnp.take_along_axis(csum, flat_e[:, None], axis=1)[:, 0] - 1
    padded = (counts + MOE_TB - 1) // MOE_TB * MOE_TB
    ends = jnp.cumsum(padded)
    pstarts = ends - padded
    dest = pstarts[flat_e] + rank
    p = a + N_EXPERTS * MOE_TB
    nblk = p // MOE_TB
    flat_tok = jnp.repeat(jnp.arange(t, dtype=jnp.int32), TOP_K)
    tok_pad = jnp.zeros((p,), jnp.int32).at[dest].set(flat_tok)
    block_e = jnp.clip(jnp.searchsorted(ends, jnp.arange(nblk, dtype=jnp.int32) * MOE_TB, side='right'),
                       0, N_EXPERTS - 1).astype(jnp.int32)
    n_valid = (ends[-1] // MOE_TB).astype(jnp.int32).reshape(1)
    dest2 = dest.reshape(t, TOP_K)
    gate_w = jnp.pad(gate, ((0, 0), (0, LANES - TOP_K)))
    return tok_pad, block_e, n_valid, dest2[:, 0], dest2[:, 1], gate_w


def _pad_cols(w, n):
    return jnp.pad(w, ((0, 0), (0, n - w.shape[1])))


def _pad_rows(w, n):
    return jnp.pad(w, ((0, n - w.shape[0]), (0, 0)))


def _rope_tables(positions):
    inv = ROPE_THETA ** (-jnp.arange(0, QK_ROPE, 2, dtype=F32) / QK_ROPE)
    ang = positions.astype(F32).reshape(-1)[:, None] * inv
    cos, sin = jnp.cos(ang), jnp.sin(ang)
    t = ang.shape[0]
    ones = jnp.ones((t, QK_NOPE), F32)
    zeros = jnp.zeros((t, QK_PAD - QK_NOPE - QK_ROPE), F32)
    cos_t = jnp.concatenate([ones, cos, cos, zeros], axis=1)
    sin_t = jnp.concatenate([jnp.zeros((t, QK_NOPE), F32), sin, sin, zeros], axis=1)
    return cos_t, sin_t


def _rot_cols(w):
    half = QK_ROPE // 2
    return jnp.concatenate([-w[..., half:], w[..., :half]], axis=-1)


def _mla_weights(q_b, kv_b):
    qb = q_b.reshape(Q_LORA, MLA_HEADS, QK_NOPE + QK_ROPE)
    zpad = jnp.zeros((Q_LORA, MLA_HEADS, QK_PAD - QK_NOPE - QK_ROPE), F32)
    wq = jnp.concatenate([qb, zpad], axis=-1).reshape(Q_LORA, MLA_HEADS * QK_PAD)
    wqr = jnp.concatenate([jnp.zeros((Q_LORA, MLA_HEADS, QK_NOPE), F32), _rot_cols(qb[..., QK_NOPE:]), zpad],
                          axis=-1).reshape(Q_LORA, MLA_HEADS * QK_PAD)
    kvb = kv_b.reshape(KV_LORA, MLA_HEADS, QK_NOPE + V_HEAD)
    wk = jnp.concatenate([kvb[..., :QK_NOPE], jnp.zeros((KV_LORA, MLA_HEADS, QK_PAD - QK_NOPE), F32)],
                         axis=-1).reshape(KV_LORA, MLA_HEADS * QK_PAD)
    wv = kvb[..., QK_NOPE:].reshape(KV_LORA, MLA_HEADS * V_HEAD).T
    eye = jnp.eye(QK_ROPE, dtype=F32)
    place = lambda blk: _pad_rows(jnp.tile(jnp.concatenate(
        [jnp.zeros((QK_ROPE, QK_NOPE), F32), blk, jnp.zeros((QK_ROPE, QK_PAD - QK_NOPE - QK_ROPE), F32)],
        axis=1), (1, MLA_HEADS)), KVR_PAD - KV_LORA)
    e_mat = place(eye)
    er_mat = place(_rot_cols(eye))
    return tuple(m.astype(BF16) for m in (wq, wqr, wk, wv, e_mat, er_mat))


def kernel(x, positions, ln0_g, ln0_b, w_in, rw_mu, rw_w0, rw_w2, rw_a0, rw_a2, rw_g2, rw_kk, rw_ka, rw_rk, rw_lnx_g, rw_lnx_b, rw_v0, rw_v1, rw_v2, mla_q_norm, mla_q_b, mla_kv_norm, mla_kv_b, conv_dw, conv_db, conv_ln_g, conv_ln_b, w_branch, w_out, ln1_g, ln1_b, router_w, router_b, moe_w13, moe_w2, ln2_g, ln2_b):
    batch, seq, d = x.shape
    t = batch * seq
    h, hb = layer_norm0(x.reshape(t, d), ln0_g, ln0_b)
    cos_t, sin_t = _rope_tables(positions)
    rw_t = router_w.T
    rw_hi = rw_t.astype(BF16)
    rw_lo = (rw_t - rw_hi.astype(F32)).astype(BF16)
    o_q = RW_SHIFT
    o_kv = o_q + Q_LORA
    o_conv = o_kv + KV_LORA + QK_ROPE
    o_gate = o_conv + 2 * CONV_W
    v_first = None
    for l in range(DEPTH):
        wi = w_in[l]
        w_rw = _pad_cols(wi[:, :RW_SHIFT], RW_PAD).astype(BF16)
        w_q = wi[:, o_q:o_kv].astype(BF16)
        w_kvr = _pad_cols(wi[:, o_kv:o_conv], KVR_PAD).astype(BF16)
        w_conv = wi[:, o_conv:o_gate].astype(BF16)
        w_gate = wi[:, o_gate:].astype(BF16)

        z_rw = matmul(hb, w_rw, 512, RW_PAD // 3)
        mu = _pad_cols(rw_mu[l].reshape(1, RW_SHIFT), RW_PAD)
        vecs = _pad_rows(jnp.stack([rw_w0[l], rw_a0[l], rw_kk[l], rw_ka[l], rw_rk[l]]), SUBLANES)
        wl = jnp.zeros((LORA_PAD, 3 * RWKV_W), F32)
        wl = wl.at[0:DECAY_LORA, 0:RWKV_W].set(rw_w2[l])
        wl = wl.at[DECAY_LORA:DECAY_LORA + AAA_LORA, RWKV_W:2 * RWKV_W].set(rw_a2[l])
        wl = wl.at[DECAY_LORA + AAA_LORA:LORA_W, 2 * RWKV_W:].set(rw_g2[l])
        wl_hi = wl.astype(BF16)
        wl_lo = (wl - wl_hi.astype(F32)).astype(BF16)
        if l == 0:
            vres = None
        else:
            vres = (v_first, rw_v0[l - 1].reshape(1, RWKV_W),
                    _pad_cols(rw_v1[l - 1], LANES).astype(BF16), _pad_rows(rw_v2[l - 1], LANES).astype(BF16))
        prep = rwkv_prep(z_rw, batch, seq, mu, vecs, wl_hi, wl_lo, vres)
        if l == 0:
            v_first = prep[6]
        y_a = rwkv_chunks(prep, batch, seq, rw_lnx_g[l], rw_lnx_b[l])

        z_q = matmul(hb, w_q, 512, Q_LORA)
        z_kvr = matmul(hb, w_kvr, 512, KVR_PAD)
        mw = _mla_weights(mla_q_b[l], mla_kv_b[l])
        q_all, k_all, v_all = mla_prep(z_q, z_kvr, cos_t, sin_t, mla_q_norm[l].reshape(1, Q_LORA),
                                       mla_kv_norm[l].reshape(1, KV_LORA), *mw)
        y_b = attention(q_all, k_all, v_all, batch, seq)

        z_conv = matmul(hb, w_conv, 512, CONV_W)
        y_c = conformer_conv(z_conv, batch, seq, _pad_rows(conv_dw[l], CONV_HALO), conv_db[l],
                             conv_ln_g[l], conv_ln_b[l])

        z_gate = [matmul(hb, w_gate[:, i * d:(i + 1) * d], 512, 1024) for i in range(N_BRANCH)]
        merged = branch_merge(y_a, y_b, y_c, z_gate, w_branch[l].astype(BF16))
        h, hb, e_rows, gate_rows = out_ln_router(merged, w_out[l].astype(BF16), h, ln1_g[l], ln1_b[l],
                                                 rw_hi, rw_lo, router_b)

        tok_pad, block_e, n_valid, p0, p1, gate_w = _route(e_rows, gate_rows)
        ys = moe_experts(hb[tok_pad], block_e, n_valid, moe_w13, moe_w2, l)
        final = l == DEPTH - 1
        res = combine_ln(h, ys[p0], ys[p1], gate_w, ln2_g[l], ln2_b[l], final)
        if final:
            h = res[0]
        else:
            h, hb = res
    return h.reshape(batch, seq, d)
```

```python
import functools

import jax
import jax.numpy as jnp
from jax import lax
from jax.experimental import pallas as pl
from jax.experimental.pallas import tpu as pltpu

F32 = jnp.float32
BF16 = jnp.bfloat16

D_MODEL = 2048
DEPTH = 4
RWKV_HEADS = 16
RWKV_HEAD = 64
RWKV_W = 1024
DECAY_LORA = 64
AAA_LORA = 64
MV_LORA = 32
GATE_LORA = 160
RWKV_GN_EPS = 64e-5
MLA_HEADS = 8
Q_LORA = 512
KV_LORA = 256
QK_NOPE = 128
QK_ROPE = 64
V_HEAD = 128
ROPE_THETA = 10000.0
ATTN_SCALE = (QK_NOPE + QK_ROPE) ** -0.5
CONV_W = 1024
CONV_K = 31
N_BRANCH = 3
N_EXPERTS = 32
N_GROUPS = 4
EXP_PER_GROUP = 8
TOP_K = 2
D_EXPERT = 512
DN_ALPHA = (2 * DEPTH) ** 0.25
LN_EPS = 1e-5
RMS_EPS = 1e-6
RW_SHIFT = 3 * RWKV_W + DECAY_LORA + AAA_LORA + GATE_LORA
LORA_W = DECAY_LORA + AAA_LORA + GATE_LORA

LANES = 128
SUBLANES = 8
VMEM_LIMIT = 48 * 1024 * 1024

LORA_PAD = 384
RW_PAD = 3 * RWKV_W + LORA_PAD
KVR_PAD = 384
QK_PAD = 256

BF16_ROWS = 16
ATTN_TILE = 512
CHUNK = 64
GROUP_W = 256
MOE_TB = 256
NEG_BIG = -1e30


def _cparams(sem):
    return pltpu.CompilerParams(dimension_semantics=sem, vmem_limit_bytes=VMEM_LIMIT)


def _dot(a, b):
    return jnp.dot(a, b, preferred_element_type=F32)


def _dot_nt(a, b):
    return lax.dot_general(a, b, (((1,), (1,)), ((), ())), preferred_element_type=F32)


def _dot_tn(a, b):
    return lax.dot_general(a, b, (((0,), (0,)), ((), ())), preferred_element_type=F32)


def _split2(x):
    hi = x.astype(BF16)
    lo = (x - hi.astype(F32)).astype(BF16)
    return hi, lo


def _split3(x):
    hi = x.astype(BF16)
    r1 = x - hi.astype(F32)
    mid = r1.astype(BF16)
    lo = (r1 - mid.astype(F32)).astype(BF16)
    return hi, mid, lo


def _dot_x2(x, w_bf16):
    hi, lo = _split2(x)
    return _dot(hi, w_bf16) + _dot(lo, w_bf16)


def _sigmoid(x):
    return 1.0 / (1.0 + jnp.exp(-x))


def _mm_kernel(a_ref, b_ref, o_ref):
    o_ref[...] = _dot(a_ref[...], b_ref[...]).astype(o_ref.dtype)


def matmul(a, b, tm, tn, out_dtype=BF16):
    m, k = a.shape
    n = b.shape[1]
    return pl.pallas_call(
        _mm_kernel,
        grid=(n // tn, m // tm),
        in_specs=[pl.BlockSpec((tm, k), lambda j, i: (i, 0)),
                  pl.BlockSpec((k, tn), lambda j, i: (0, j))],
        out_specs=pl.BlockSpec((tm, tn), lambda j, i: (i, j)),
        out_shape=jax.ShapeDtypeStruct((m, n), out_dtype),
        compiler_params=_cparams(("parallel", "parallel")),
        name="matmul",
    )(a, b)


def _ln_rows(x, g, b):
    mu = jnp.mean(x, axis=-1, keepdims=True)
    xc = x - mu
    var = jnp.mean(xc * xc, axis=-1, keepdims=True)
    return xc * lax.rsqrt(var + LN_EPS) * g + b


def _ln0_kernel(x_ref, g_ref, b_ref, h_ref, hb_ref):
    h = _ln_rows(x_ref[...], g_ref[...], b_ref[...])
    h_ref[...] = h
    hb_ref[...] = h.astype(BF16)


def layer_norm0(x2, g, b, tm=256):
    t, d = x2.shape
    row = pl.BlockSpec((tm, d), lambda i: (i, 0))
    vec = pl.BlockSpec((1, d), lambda i: (0, 0))
    return pl.pallas_call(
        _ln0_kernel, grid=(t // tm,), in_specs=[row, vec, vec], out_specs=[row, row],
        out_shape=[jax.ShapeDtypeStruct((t, d), F32), jax.ShapeDtypeStruct((t, d), BF16)],
        compiler_params=_cparams(("parallel",)), name="ln0",
    )(x2, g.reshape(1, d), b.reshape(1, d))


def _rwkv_prep_kernel(has_vres, *refs):
    if has_vres:
        (z_ref, zp_ref, mu_ref, vec_ref, wa_hi_ref, wa_lo_ref, wg_hi_ref, wg_lo_ref, bd_ref, tri_ref, ones_ref,
         vf_ref, v0_ref, v1_ref, v2_ref,
         rt_ref, kap_ref, bt_ref, kt_ref, bh_ref, kh_ref, v_ref, bonus_ref, g_ref, pl8_ref) = refs
    else:
        (z_ref, zp_ref, mu_ref, vec_ref, wa_hi_ref, wa_lo_ref, wg_hi_ref, wg_lo_ref, bd_ref, tri_ref, ones_ref,
         rt_ref, kap_ref, bt_ref, kt_ref, bh_ref, kh_ref, v_ref, bonus_ref, g_ref, pl8_ref) = refs
    tm = z_ref.shape[0]
    first = pl.program_id(1) == 0

    z = z_ref[...].astype(F32)
    prev = zp_ref[SUBLANES - 1:SUBLANES, :].astype(F32)
    prev = jnp.where(first, 0.0, prev)
    row = lax.broadcasted_iota(jnp.int32, z.shape, 0)
    zs = jnp.where(row == 0, prev, pltpu.roll(z, 1, 0))
    z = z + (zs - z) * mu_ref[...]

    r = z[:, 0:RWKV_W]
    k = z[:, RWKV_W:2 * RWKV_W]
    v = z[:, 2 * RWKV_W:3 * RWKV_W]
    zl = z[:, 3 * RWKV_W:RW_PAD]
    col = lax.broadcasted_iota(jnp.int32, zl.shape, 1)
    act = jnp.where(col < DECAY_LORA, jnp.tanh(zl),
                    jnp.where(col < DECAY_LORA + AAA_LORA, zl, _sigmoid(zl)))

    def lora_dot(a, hi_ref, lo_ref):
        a_hi, a_lo = _split2(a)
        hi = hi_ref[...]
        return _dot(a_hi, hi) + _dot(a_lo, hi) + _dot(a_hi, lo_ref[...])

    lora_wa = lora_dot(act[:, :LANES], wa_hi_ref, wa_lo_ref)
    g = lora_dot(act[:, LANES:], wg_hi_ref, wg_lo_ref)
    w0 = vec_ref[0:1, :]
    a0 = vec_ref[1:2, :]
    k_k = vec_ref[2:3, :]
    k_a = vec_ref[3:4, :]
    r_k = vec_ref[4:5, :]
    x = -(w0 + lora_wa[:, 0:RWKV_W])
    softplus = jnp.maximum(x, 0.0) + jnp.log(1.0 + jnp.exp(-jnp.abs(x)))
    logd = -jnp.exp(-softplus - 0.5)
    a = _sigmoid(a0 + lora_wa[:, RWKV_W:])

    if has_vres:
        lo_rank = _dot(_dot(v.astype(BF16), v1_ref[...]).astype(BF16), v2_ref[...])
        v = v + (vf_ref[...].astype(F32) - v) * _sigmoid(v0_ref[...] + lo_rank)

    bd = bd_ref[...]

    def seg_sum(t):
        return jnp.concatenate(
            [_dot_x2(t[:, c:c + GROUP_W], bd) for c in range(0, RWKV_W, GROUP_W)], axis=1)

    kk = k * k_k
    kk = kk * lax.rsqrt(jnp.maximum(seg_sum(kk * kk), 1e-24))
    kmod = k * (1.0 + (a - 1.0) * k_a)
    b = kk * a
    bonus = seg_sum(r * kmod * r_k) * v

    l_hi, l_mid, l_lo = _split3(logd)
    tri = tri_ref[...]
    ones = ones_ref[...]
    c = _dot(tri, l_hi) + _dot(tri, l_mid) + _dot(tri, l_lo)
    cl = _dot(ones, l_hi) + _dot(ones, l_mid) + _dot(ones, l_lo)
    e_c = jnp.exp(c)
    e_prev = jnp.exp(c - logd)
    e_inv = jnp.exp(-c)
    e_rest = jnp.exp(cl - c)
    rt_ref[...] = (r * e_c).astype(BF16)
    kap_ref[...] = (kk * e_prev).astype(BF16)
    bt_ref[...] = (b * e_inv).astype(BF16)
    kt_ref[...] = (kmod * e_inv).astype(BF16)
    bh_ref[...] = (b * e_rest).astype(BF16)
    kh_ref[...] = (kmod * e_rest).astype(BF16)
    v_ref[...] = v.astype(BF16)
    bonus_ref[...] = bonus.astype(BF16)
    g_ref[...] = g.astype(BF16)
    e_cl = jnp.exp(cl)
    pl8_ref[...] = jnp.concatenate(
        [e_cl[j * CHUNK:j * CHUNK + SUBLANES] for j in range(tm // CHUNK)], axis=0)


def rwkv_prep(z_rw, batch, seq, mu, vecs, lora_w, vres, tm=256):
    t = z_rw.shape[0]
    nb = seq // tm
    has_vres = vres is not None
    ii = lax.broadcasted_iota(jnp.int32, (GROUP_W, GROUP_W), 0)
    jj = lax.broadcasted_iota(jnp.int32, (GROUP_W, GROUP_W), 1)
    bd = (ii // RWKV_HEAD == jj // RWKV_HEAD).astype(BF16)
    it = lax.broadcasted_iota(jnp.int32, (tm, tm), 0)
    jt = lax.broadcasted_iota(jnp.int32, (tm, tm), 1)
    same = it // CHUNK == jt // CHUNK
    tri = (same & (jt <= it)).astype(BF16)
    ones = same.astype(BF16)

    zrow = pl.BlockSpec((tm, RW_PAD), lambda b, i: (b * nb + i, 0))
    zprev = pl.BlockSpec(
        (SUBLANES, RW_PAD), lambda b, i: (jnp.maximum((b * nb + i) * (tm // SUBLANES) - 1, 0), 0))
    full = lambda shp: pl.BlockSpec(shp, lambda b, i: (0,) * len(shp))
    wrow = pl.BlockSpec((tm, RWKV_W), lambda b, i: (b * nb + i, 0))
    in_specs = [zrow, zprev, full((1, RW_PAD)), full((SUBLANES, RWKV_W)),
                full((LANES, 2 * RWKV_W)), full((LANES, 2 * RWKV_W)),
                full((LORA_PAD - LANES, RWKV_W)), full((LORA_PAD - LANES, RWKV_W)),
                full((GROUP_W, GROUP_W)), full((tm, tm)), full((tm, tm))]
    args = [z_rw, z_rw, mu, vecs, *lora_w, bd, tri, ones]
    if has_vres:
        v_first, v0, v1, v2 = vres
        in_specs += [wrow, full((1, RWKV_W)), full((RWKV_W, LANES)), full((LANES, RWKV_W))]
        args += [v_first, v0, v1, v2]
    p8 = tm // CHUNK * SUBLANES
    out_specs = [wrow] * 9 + [pl.BlockSpec((p8, RWKV_W), lambda b, i: (b * nb + i, 0))]
    out_shape = [jax.ShapeDtypeStruct((t, RWKV_W), BF16)] * 9 + [
        jax.ShapeDtypeStruct((t // CHUNK * SUBLANES, RWKV_W), F32)]
    return pl.pallas_call(
        functools.partial(_rwkv_prep_kernel, has_vres),
        grid=(batch, nb), in_specs=in_specs, out_specs=out_specs, out_shape=out_shape,
        compiler_params=_cparams(("parallel", "parallel")), name="rwkv_prep",
    )(*args)


def _rwkv_chunk_kernel(rt_ref, kap_ref, bt_ref, kt_ref, bh_ref, kh_ref, v_ref, pl8_ref,
                       bonus_ref, g_ref, lng_ref, lnb_ref, y_ref, h_ref):
    tb = rt_ref.shape[0]
    w = GROUP_W

    @pl.when(pl.program_id(2) == 0)
    def _():
        h_ref[...] = jnp.zeros_like(h_ref)

    ri = lax.broadcasted_iota(jnp.int32, (w, w), 0)
    ci = lax.broadcasted_iota(jnp.int32, (w, w), 1)
    bd_mask = (ri >> 6) == (ci >> 6)
    eye = ri == ci
    tr = lax.broadcasted_iota(jnp.int32, (CHUNK, w), 0)
    sc = lax.broadcasted_iota(jnp.int32, (CHUNK, w), 1) & (CHUNK - 1)
    strict = sc < tr
    incl = sc <= tr
    seg_mean = jnp.where(bd_mask, 1.0 / RWKV_HEAD, 0.0).astype(BF16)

    def bd(y):
        return jnp.where(bd_mask, jnp.concatenate([y] * (w // CHUNK), axis=0), jnp.zeros((), y.dtype))

    def mm(x, y_bd):
        return _dot(x.astype(BF16), y_bd)

    cs = range(tb // CHUNK)
    rows = [slice(c * CHUNK, (c + 1) * CHUNK) for c in cs]
    rt = [rt_ref[r, :] for r in rows]
    kap = [kap_ref[r, :] for r in rows]
    v = [v_ref[r, :] for r in rows]
    bh = [bh_ref[r, :] for r in rows]
    lhs = [jnp.concatenate([kap[c], rt[c]], axis=0) for c in cs]
    a_b = [_dot_nt(lhs[c], bd(bt_ref[rows[c], :])) for c in cs]
    a_k = [_dot_nt(lhs[c], bd(kt_ref[rows[c], :])) for c in cs]
    a_rb = [jnp.where(incl, a_b[c][CHUNK:], 0.0).astype(BF16) for c in cs]
    a_kk = [jnp.concatenate([jnp.where(strict, a_k[c][:CHUNK], 0.0),
                             jnp.where(incl, a_k[c][CHUNK:], 0.0)], axis=0) for c in cs]

    n_pow = [-jnp.where(strict, a_b[c][:CHUNK], 0.0) for c in cs]
    t_inv = [jnp.where(sc == tr, 1.0, 0.0) + n_pow[c] for c in cs]
    n_pow = [mm(n_pow[c], bd(n_pow[c].astype(BF16))) for c in cs]
    for _ in range(4):
        both = [mm(jnp.concatenate([n_pow[c], t_inv[c]], axis=0), bd(n_pow[c].astype(BF16))) for c in cs]
        n_pow = [both[c][:CHUNK] for c in cs]
        t_inv = [t_inv[c] + both[c][CHUNK:] for c in cs]
    t_inv = [t_inv[c] + mm(t_inv[c], bd(n_pow[c].astype(BF16))) for c in cs]

    av = [mm(a_kk[c], bd(v[c])) for c in cs]
    t_bf = [t_inv[c].astype(BF16) for c in cs]
    tk = [_dot(t_bf[c], jnp.concatenate([bd(kap[c]), bd(av[c][:CHUNK].astype(BF16))], axis=1)).astype(BF16)
          for c in cs]
    ar = [_dot(a_rb[c], jnp.concatenate([bd(tk[c][:, :w]), bd(tk[c][:, w:])], axis=1)) for c in cs]
    r_hat = [(rt[c].astype(F32) - ar[c][:, :w]).astype(BF16) for c in cs]
    y_in = [av[c][CHUNK:] - ar[c][:, w:] for c in cs]
    lhs2 = [jnp.concatenate([r_hat[c], tk[c][:, :w]], axis=0) for c in cs]
    s_t = [jnp.concatenate([kh_ref[rows[c], :], bh[c]], axis=0).astype(F32).T.astype(BF16) for c in cs]
    pl_col = [jnp.sum(jnp.where(eye, pl8_ref[c * SUBLANES:c * SUBLANES + 1, :], 0.0), axis=1, keepdims=True)
              for c in cs]

    h = h_ref[...]
    y = []
    for c in cs:
        h_hi, h_lo = _split2(h)
        uy = _dot(lhs2[c], h_hi) + _dot(lhs2[c], h_lo)
        y.append(y_in[c] + uy[:CHUNK])
        r_mat = jnp.concatenate([v[c], (-(tk[c][:, w:].astype(F32) + uy[CHUNK:])).astype(BF16)], axis=0)
        h = h * pl_col[c] + jnp.where(bd_mask, _dot(s_t[c], r_mat), 0.0)
    h_ref[...] = h

    y_all = jnp.concatenate(y, axis=0)
    mu = _dot_x2(y_all, seg_mean)
    yc = y_all - mu
    var = _dot_x2(yc * yc, seg_mean)
    out = yc * lax.rsqrt(var + RWKV_GN_EPS) * lng_ref[...] + lnb_ref[...]
    out = (out + bonus_ref[...].astype(F32)) * g_ref[...].astype(F32)
    y_ref[...] = out.astype(y_ref.dtype)


def rwkv_chunks(prep, batch, seq, lnx_g, lnx_b, tb=512):
    rt, kap, bt, kt, bh, kh, v, bonus, g, pl8 = prep
    t = rt.shape[0]
    nb = seq // tb
    ng = RWKV_W // GROUP_W
    blk = pl.BlockSpec((tb, GROUP_W), lambda b, q, i: (b * nb + i, q))
    p8 = pl.BlockSpec((tb // CHUNK * SUBLANES, GROUP_W), lambda b, q, i: (b * nb + i, q))
    vec = pl.BlockSpec((1, GROUP_W), lambda b, q, i: (0, q))
    return pl.pallas_call(
        _rwkv_chunk_kernel,
        grid=(batch, ng, nb),
        in_specs=[blk] * 7 + [p8, blk, blk, vec, vec],
        out_specs=blk,
        out_shape=jax.ShapeDtypeStruct((t, RWKV_W), BF16),
        scratch_shapes=[pltpu.VMEM((GROUP_W, GROUP_W), F32)],
        compiler_params=_cparams(("parallel", "parallel", "arbitrary")), name="rwkv_chunks",
    )(rt, kap, bt, kt, bh, kh, v, pl8, bonus, g, lnx_g.reshape(1, RWKV_W), lnx_b.reshape(1, RWKV_W))


def _mla_prep_kernel(zq_ref, zkv_ref, cos_ref, sin_ref, qn_ref, kvn_ref, wq_ref, wqr_ref,
                     wk_ref, wv_ref, e_ref, er_ref, q_ref, k_ref, v_ref):
    zq = zq_ref[...].astype(F32)
    qn = (zq * lax.rsqrt(jnp.mean(zq * zq, axis=-1, keepdims=True) + RMS_EPS) * qn_ref[...]).astype(BF16)
    zkv = zkv_ref[:, 0:KV_LORA].astype(F32)
    cn = (zkv * lax.rsqrt(jnp.mean(zkv * zkv, axis=-1, keepdims=True) + RMS_EPS) * kvn_ref[...]).astype(BF16)
    kr = zkv_ref[:, KV_LORA:KVR_PAD]
    cos = cos_ref[...]
    sin = sin_ref[...]
    q_all = _dot(qn, wq_ref[...])
    q_rot = _dot(qn, wqr_ref[...])
    k_all = _dot(cn, wk_ref[...]) + _dot(kr, e_ref[...])
    k_rot = _dot(kr, er_ref[...])
    for h in range(MLA_HEADS):
        cols = slice(h * QK_PAD, (h + 1) * QK_PAD)
        q_ref[:, cols] = ((q_all[:, cols] * cos + q_rot[:, cols] * sin) * ATTN_SCALE).astype(BF16)
        k_ref[:, cols] = (k_all[:, cols] * cos + k_rot[:, cols] * sin).astype(BF16)
    v_ref[...] = _dot_nt(wv_ref[...], cn).astype(BF16)


def mla_prep(z_q, z_kvr, cos_t, sin_t, qn, kvn, wq, wqr, wk, wv_t, e_mat, er_mat, tm=ATTN_TILE):
    t = z_q.shape[0]
    hw = MLA_HEADS * QK_PAD
    vw = MLA_HEADS * V_HEAD
    row = lambda w: pl.BlockSpec((tm, w), lambda i: (i, 0))
    full = lambda a: pl.BlockSpec(a.shape, lambda i: (0, 0))
    return pl.pallas_call(
        _mla_prep_kernel, grid=(t // tm,),
        in_specs=[row(Q_LORA), row(KVR_PAD), row(QK_PAD), row(QK_PAD), full(qn), full(kvn),
                  full(wq), full(wqr), full(wk), full(wv_t), full(e_mat), full(er_mat)],
        out_specs=[row(hw), row(hw), pl.BlockSpec((None, vw, tm), lambda i: (i, 0, 0))],
        out_shape=[jax.ShapeDtypeStruct((t, hw), BF16), jax.ShapeDtypeStruct((t, hw), BF16),
                   jax.ShapeDtypeStruct((t // tm, vw, tm), BF16)],
        compiler_params=_cparams(("parallel",)), name="mla_prep",
    )(z_q, z_kvr, cos_t, sin_t, qn, kvn, wq, wqr, wk, wv_t, e_mat, er_mat)


def _attn_kernel(q_ref, k_ref, vt_ref, o_ref, st0_ref, st1_ref, mx0_ref, mx1_ref, m_ref, acc_ref):
    tq = q_ref.shape[0]
    tk = tq
    i = pl.program_id(2)
    q = q_ref[...]
    ones = jnp.ones((BF16_ROWS, tk), BF16)

    def qk(j, st_ref, mx_ref):
        start = pl.multiple_of(j * tk, tk)
        st = _dot_nt(k_ref[pl.ds(start, tk), :], q)
        st_ref[...] = st
        mx_ref[...] = jnp.max(st, axis=0, keepdims=True)

    def consume(j, st_ref, mx_ref, masked):
        st = st_ref[...]
        if masked:
            ki = lax.broadcasted_iota(jnp.int32, st.shape, 0)
            qi = lax.broadcasted_iota(jnp.int32, st.shape, 1)
            st = jnp.where(ki <= qi, st, NEG_BIG)
            mx = jnp.max(st, axis=0, keepdims=True)
        else:
            mx = mx_ref[...]
        m = m_ref[...]
        m_new = jnp.maximum(m, mx)
        alpha = jnp.exp(m - m_new)
        p = jnp.exp(st - m_new).astype(BF16)
        vt = jnp.concatenate([vt_ref[j], ones], axis=0)
        acc_ref[...] = alpha * acc_ref[...] + _dot(vt, p)
        m_ref[...] = m_new

    m_ref[...] = jnp.full(m_ref.shape, NEG_BIG, F32)
    acc_ref[...] = jnp.zeros(acc_ref.shape, F32)
    qk(0, st0_ref, mx0_ref)

    def pair(jj, carry):
        j = 2 * jj
        qk(j + 1, st1_ref, mx1_ref)
        consume(j, st0_ref, mx0_ref, False)
        qk(j + 2, st0_ref, mx0_ref)
        consume(j + 1, st1_ref, mx1_ref, False)
        return carry

    lax.fori_loop(0, i // 2, pair, 0)

    @pl.when(i % 2 == 0)
    def _():
        consume(i, st0_ref, mx0_ref, True)

    @pl.when(i % 2 == 1)
    def _():
        qk(i, st1_ref, mx1_ref)
        consume(i - 1, st0_ref, mx0_ref, False)
        consume(i, st1_ref, mx1_ref, True)

    acc = acc_ref[...]
    out = acc[:V_HEAD] / acc[V_HEAD:V_HEAD + 1]
    o_ref[...] = out.T.astype(o_ref.dtype)


def attention(q_all, k_all, vt_all, batch, seq, tq=ATTN_TILE):
    t = q_all.shape[0]
    nq = seq // tq
    return pl.pallas_call(
        _attn_kernel,
        grid=(batch, MLA_HEADS, nq),
        in_specs=[pl.BlockSpec((tq, QK_PAD), lambda b, h, i: (b * nq + i, h)),
                  pl.BlockSpec((seq, QK_PAD), lambda b, h, i: (b, h)),
                  pl.BlockSpec((nq, V_HEAD, tq), lambda b, h, i: (b, h, 0))],
        out_specs=pl.BlockSpec((tq, V_HEAD), lambda b, h, i: (b * nq + i, h)),
        out_shape=jax.ShapeDtypeStruct((t, MLA_HEADS * V_HEAD), BF16),
        scratch_shapes=[pltpu.VMEM((tq, tq), F32), pltpu.VMEM((tq, tq), F32),
                        pltpu.VMEM((1, tq), F32), pltpu.VMEM((1, tq), F32),
                        pltpu.VMEM((1, tq), F32), pltpu.VMEM((V_HEAD + BF16_ROWS, tq), F32)],
        compiler_params=_cparams(("parallel", "parallel", "arbitrary")), name="attention",
    )(q_all, k_all, vt_all)


CONV_HALO = 32


def _conv_kernel(z_ref, zh_ref, dw_ref, db_ref, g_ref, b_ref, o_ref, u_ref, s_ref):
    tm = z_ref.shape[0]
    first = pl.program_id(1) == 0

    def glu(zz):
        zz = zz.astype(F32)
        return zz[:, :CONV_W] * _sigmoid(zz[:, CONV_W:])

    u_ref[0:CONV_HALO, :] = jnp.where(first, 0.0, glu(zh_ref[...]))
    u_ref[CONV_HALO:, :] = glu(z_ref[...])
    off = CONV_HALO - (CONV_K - 1)
    acc = jnp.zeros((tm, CONV_W), F32) + db_ref[...]
    for res in range(SUBLANES):
        taps = [j for j in range(CONV_K) if (off + j) % SUBLANES == res]
        span = max(off + j for j in taps) - res + tm
        if res == 0:
            src_ref = u_ref
        else:
            s_ref[0:span, :] = u_ref[res:res + span, :]
            src_ref = s_ref
        for j in taps:
            lo = off + j - res
            acc = acc + dw_ref[j:j + 1, :] * src_ref[lo:lo + tm, :]
    y = _ln_rows(acc, g_ref[...], b_ref[...])
    o_ref[...] = (y * _sigmoid(y)).astype(o_ref.dtype)


def conformer_conv(z_conv, batch, seq, dw, db, ln_g, ln_b, tm=256):
    t = z_conv.shape[0]
    nb = seq // tm
    ratio = tm // CONV_HALO
    vec = pl.BlockSpec((1, CONV_W), lambda b, i: (0, 0))
    return pl.pallas_call(
        _conv_kernel, grid=(batch, nb),
        in_specs=[pl.BlockSpec((tm, 2 * CONV_W), lambda b, i: (b * nb + i, 0)),
                  pl.BlockSpec((CONV_HALO, 2 * CONV_W),
                               lambda b, i: (jnp.maximum((b * nb + i) * ratio - 1, 0), 0)),
                  pl.BlockSpec((CONV_HALO, CONV_W), lambda b, i: (0, 0)), vec, vec, vec],
        out_specs=pl.BlockSpec((tm, CONV_W), lambda b, i: (b * nb + i, 0)),
        out_shape=jax.ShapeDtypeStruct((t, CONV_W), BF16),
        scratch_shapes=[pltpu.VMEM((tm + CONV_HALO, CONV_W), F32), pltpu.VMEM((tm + CONV_HALO, CONV_W), F32)],
        compiler_params=_cparams(("parallel", "parallel")), name="conformer_conv",
    )(z_conv, z_conv, dw, db.reshape(1, CONV_W), ln_g.reshape(1, CONV_W), ln_b.reshape(1, CONV_W))


def _merge_kernel(ya_ref, yb_ref, yc_ref, ga_ref, gb_ref, gc_ref, wb_ref, o_ref):
    acc = None
    for i, (y_ref, zg_ref) in enumerate(((ya_ref, ga_ref), (yb_ref, gb_ref), (yc_ref, gc_ref))):
        term = _sigmoid(zg_ref[...].astype(F32)) * _dot(y_ref[...], wb_ref[i])
        acc = term if acc is None else acc + term
    o_ref[...] = acc.astype(o_ref.dtype)


def branch_merge(y_a, y_b, y_c, zg, wb, tm=512, tn=1024):
    t = y_a.shape[0]
    yspec = pl.BlockSpec((tm, RWKV_W), lambda j, i: (i, 0))
    gspec = pl.BlockSpec((tm, tn), lambda j, i: (i, j))
    return pl.pallas_call(
        _merge_kernel, grid=(D_MODEL // tn, t // tm),
        in_specs=[yspec, yspec, yspec, gspec, gspec, gspec,
                  pl.BlockSpec((N_BRANCH, RWKV_W, tn), lambda j, i: (0, 0, j))],
        out_specs=pl.BlockSpec((tm, tn), lambda j, i: (i, j)),
        out_shape=jax.ShapeDtypeStruct((t, D_MODEL), BF16),
        compiler_params=_cparams(("parallel", "parallel")), name="branch_merge",
    )(y_a, y_b, y_c, *zg, wb)


def _out_ln_router_kernel(m_ref, w_ref, h_ref, g_ref, b_ref, rw_hi_ref, rw_lo_ref, rb_ref, tri_ref,
                          ho_ref, hb_ref, ei_ref, gw_ref, cnt_out_ref, cnt_ref):
    x = DN_ALPHA * h_ref[...] + _dot(m_ref[...], w_ref[...])
    h = _ln_rows(x, g_ref[...], b_ref[...])
    ho_ref[...] = h
    hb_ref[...] = h.astype(BF16)
    h_hi, h_lo = _split2(h)
    rw_hi = rw_hi_ref[...]
    logits = _dot_nt(rw_hi, h_hi) + _dot_nt(rw_hi, h_lo) + _dot_nt(rw_lo_ref[...], h_hi)
    scores = _sigmoid(logits)
    sel = scores + rb_ref[...]
    tm = sel.shape[1]
    idx = lax.broadcasted_iota(jnp.int32, (EXP_PER_GROUP, tm), 0)

    def first_max(x):
        mx = jnp.max(x, axis=0, keepdims=True)
        return mx, jnp.min(jnp.where(x == mx, idx, EXP_PER_GROUP), axis=0, keepdims=True)

    best = None
    for grp in range(N_GROUPS):
        rows = slice(grp * EXP_PER_GROUP, (grp + 1) * EXP_PER_GROUP)
        x = sel[rows]
        sc_g = scores[rows]
        m1, i1 = first_max(x)
        m2, i2 = first_max(jnp.where(idx == i1, -jnp.inf, x))
        s1 = jnp.sum(jnp.where(idx == i1, sc_g, 0.0), axis=0, keepdims=True)
        s2 = jnp.sum(jnp.where(idx == i2, sc_g, 0.0), axis=0, keepdims=True)
        cand = (m1 + m2, i1 + grp * EXP_PER_GROUP, i2 + grp * EXP_PER_GROUP, s1, s2)
        if best is None:
            best = cand
        else:
            take = cand[0] > best[0]
            best = tuple(jnp.where(take, cn, bs) for cn, bs in zip(cand, best))
    _, e1, e2, s1, s2 = best
    @pl.when(pl.program_id(0) == 0)
    def _():
        cnt_ref[...] = jnp.zeros_like(cnt_ref)

    erow = lax.broadcasted_iota(jnp.int32, (N_EXPERTS, tm), 0)
    hit1 = erow == e1
    hit2 = erow == e2
    onehot = jnp.where(hit1, 1.0, jnp.where(hit2, 1.0, 0.0))
    before = cnt_ref[:, 0:1] + _dot(onehot.astype(BF16), tri_ref[...]) - onehot
    rank1 = jnp.sum(jnp.where(hit1, before, 0.0), axis=0, keepdims=True).astype(jnp.int32)
    rank2 = jnp.sum(jnp.where(hit2, before, 0.0), axis=0, keepdims=True).astype(jnp.int32)
    cnt_ref[...] = cnt_ref[...] + jnp.sum(onehot, axis=1, keepdims=True)
    cnt_out_ref[...] = cnt_ref[...]
    row = lax.broadcasted_iota(jnp.int32, (SUBLANES, tm), 0)
    ei_ref[...] = jnp.where(row == 0, e1, jnp.where(row == 1, e2,
                                                    jnp.where(row == 2, rank1, jnp.where(row == 3, rank2, 0))))
    inv = 1.0 / (s1 + s2)
    gw_ref[...] = jnp.where(row == 0, s1 * inv, jnp.where(row == 1, s2 * inv, 0.0))


def out_ln_router(merged, w_out, h, g, b, rw_hi, rw_lo, rb, tm=256):
    t, d = h.shape
    si = lax.broadcasted_iota(jnp.int32, (tm, tm), 0)
    ti = lax.broadcasted_iota(jnp.int32, (tm, tm), 1)
    tri = (si <= ti).astype(BF16)
    row = pl.BlockSpec((tm, d), lambda i: (i, 0))
    vec = pl.BlockSpec((1, d), lambda i: (0, 0))
    rws = pl.BlockSpec((N_EXPERTS, d), lambda i: (0, 0))
    tok = pl.BlockSpec((SUBLANES, tm), lambda i: (0, i))
    return pl.pallas_call(
        _out_ln_router_kernel, grid=(t // tm,),
        in_specs=[row, pl.BlockSpec((d, d), lambda i: (0, 0)), row, vec, vec, rws, rws,
                  pl.BlockSpec((N_EXPERTS, tm), lambda i: (0, 0)), pl.BlockSpec((tm, tm), lambda i: (0, 0))],
        out_specs=[row, row, tok, tok, pl.BlockSpec((N_EXPERTS, LANES), lambda i: (0, 0))],
        out_shape=[jax.ShapeDtypeStruct((t, d), F32), jax.ShapeDtypeStruct((t, d), BF16),
                   jax.ShapeDtypeStruct((SUBLANES, t), jnp.int32), jax.ShapeDtypeStruct((SUBLANES, t), F32),
                   jax.ShapeDtypeStruct((N_EXPERTS, LANES), F32)],
        scratch_shapes=[pltpu.VMEM((N_EXPERTS, LANES), F32)],
        compiler_params=_cparams(("arbitrary",)), name="out_ln_router",
    )(merged, w_out, h, g.reshape(1, d), b.reshape(1, d), rw_hi, rw_lo,
      jnp.broadcast_to(rb.reshape(N_EXPERTS, 1), (N_EXPERTS, tm)), tri)


def _moe_kernel(be_ref, nv_ref, x_ref, w13_ref, w2_ref, o_ref, w13b_ref, w2b_ref):
    i = pl.program_id(0)

    @pl.when((i == 0) | (be_ref[i] != be_ref[jnp.maximum(i - 1, 0)]))
    def _():
        w13b_ref[...] = w13_ref[...].astype(BF16)
        w2b_ref[...] = w2_ref[...].astype(BF16)

    @pl.when(i < nv_ref[0])
    def _():
        gu = _dot(x_ref[...], w13b_ref[...])
        gate = gu[:, :D_EXPERT]
        act = gate * _sigmoid(gate) * gu[:, D_EXPERT:]
        o_ref[...] = _dot(act.astype(BF16), w2b_ref[...]).astype(o_ref.dtype)

    @pl.when(i >= nv_ref[0])
    def _():
        o_ref[...] = jnp.zeros_like(o_ref)


def moe_experts(x_sorted, block_e, n_valid, w13, w2, layer):
    p, d = x_sorted.shape
    nblk = p // MOE_TB
    grid_spec = pltpu.PrefetchScalarGridSpec(
        num_scalar_prefetch=2, grid=(nblk,),
        in_specs=[pl.BlockSpec((MOE_TB, d), lambda i, be, nv: (i, 0)),
                  pl.BlockSpec((None, None, d, 2 * D_EXPERT), lambda i, be, nv: (layer, be[i], 0, 0)),
                  pl.BlockSpec((None, None, D_EXPERT, d), lambda i, be, nv: (layer, be[i], 0, 0))],
        out_specs=pl.BlockSpec((MOE_TB, d), lambda i, be, nv: (i, 0)),
        scratch_shapes=[pltpu.VMEM((d, 2 * D_EXPERT), BF16), pltpu.VMEM((D_EXPERT, d), BF16)])
    return pl.pallas_call(
        _moe_kernel, grid_spec=grid_spec,
        out_shape=jax.ShapeDtypeStruct((p, d), BF16),
        compiler_params=_cparams(("arbitrary",)), name="moe_experts",
    )(block_e, n_valid, x_sorted, w13, w2)


def _combine_ln_kernel(final, h_ref, y0_ref, y1_ref, gw_ref, g_ref, b_ref, *outs):
    gw = gw_ref[...]
    f = gw[:, 0:1] * y0_ref[...].astype(F32) + gw[:, 1:2] * y1_ref[...].astype(F32)
    h = _ln_rows(DN_ALPHA * h_ref[...] + f, g_ref[...], b_ref[...])
    outs[0][...] = h
    if not final:
        outs[1][...] = h.astype(BF16)


def combine_ln(h, y0, y1, gate_w, g, b, final, tm=256):
    t, d = h.shape
    row = pl.BlockSpec((tm, d), lambda i: (i, 0))
    vec = pl.BlockSpec((1, d), lambda i: (0, 0))
    out_specs = [row] if final else [row, row]
    out_shape = [jax.ShapeDtypeStruct((t, d), F32)] + ([] if final else [jax.ShapeDtypeStruct((t, d), BF16)])
    return pl.pallas_call(
        functools.partial(_combine_ln_kernel, final), grid=(t // tm,),
        in_specs=[row, row, row, pl.BlockSpec((tm, LANES), lambda i: (i, 0)), vec, vec],
        out_specs=out_specs, out_shape=out_shape,
        compiler_params=_cparams(("parallel",)), name="combine_ln",
    )(h, y0, y1, gate_w, g.reshape(1, d), b.reshape(1, d))


def _route(e_rows, gate_rows, counts):
    e_idx = e_rows[:TOP_K].T
    rank = e_rows[TOP_K:2 * TOP_K].T.reshape(-1)
    gate = gate_rows[:TOP_K].T
    t = e_idx.shape[0]
    a = t * TOP_K
    flat_e = e_idx.reshape(a)
    counts = counts[:, 0].astype(jnp.int32)
    padded = (counts + MOE_TB - 1) // MOE_TB * MOE_TB
    ends = jnp.cumsum(padded)
    pstarts = ends - padded
    dest = pstarts[flat_e] + rank
    p = a + N_EXPERTS * MOE_TB
    nblk = p // MOE_TB
    flat_tok = jnp.repeat(jnp.arange(t, dtype=jnp.int32), TOP_K)
    tok_pad = jnp.zeros((p,), jnp.int32).at[dest].set(flat_tok)
    block_e = jnp.clip(jnp.searchsorted(ends, jnp.arange(nblk, dtype=jnp.int32) * MOE_TB, side='right'),
                       0, N_EXPERTS - 1).astype(jnp.int32)
    n_valid = (ends[-1] // MOE_TB).astype(jnp.int32).reshape(1)
    dest2 = dest.reshape(t, TOP_K)
    gate_w = jnp.pad(gate, ((0, 0), (0, LANES - TOP_K)))
    return tok_pad, block_e, n_valid, dest2[:, 0], dest2[:, 1], gate_w


def _pad_cols(w, n):
    return jnp.pad(w, ((0, 0), (0, n - w.shape[1])))


def _pad_rows(w, n):
    return jnp.pad(w, ((0, n - w.shape[0]), (0, 0)))


def _rope_tables(positions):
    inv = ROPE_THETA ** (-jnp.arange(0, QK_ROPE, 2, dtype=F32) / QK_ROPE)
    ang = positions.astype(F32).reshape(-1)[:, None] * inv
    cos, sin = jnp.cos(ang), jnp.sin(ang)
    t = ang.shape[0]
    ones = jnp.ones((t, QK_NOPE), F32)
    zeros = jnp.zeros((t, QK_PAD - QK_NOPE - QK_ROPE), F32)
    cos_t = jnp.concatenate([ones, cos, cos, zeros], axis=1)
    sin_t = jnp.concatenate([jnp.zeros((t, QK_NOPE), F32), sin, sin, zeros], axis=1)
    return cos_t, sin_t


def _rot_cols(w):
    half = QK_ROPE // 2
    return jnp.concatenate([-w[..., half:], w[..., :half]], axis=-1)


def _mla_weights(q_b, kv_b):
    qb = q_b.reshape(Q_LORA, MLA_HEADS, QK_NOPE + QK_ROPE)
    zpad = jnp.zeros((Q_LORA, MLA_HEADS, QK_PAD - QK_NOPE - QK_ROPE), F32)
    wq = jnp.concatenate([qb, zpad], axis=-1).reshape(Q_LORA, MLA_HEADS * QK_PAD)
    wqr = jnp.concatenate([jnp.zeros((Q_LORA, MLA_HEADS, QK_NOPE), F32), _rot_cols(qb[..., QK_NOPE:]), zpad],
                          axis=-1).reshape(Q_LORA, MLA_HEADS * QK_PAD)
    kvb = kv_b.reshape(KV_LORA, MLA_HEADS, QK_NOPE + V_HEAD)
    wk = jnp.concatenate([kvb[..., :QK_NOPE], jnp.zeros((KV_LORA, MLA_HEADS, QK_PAD - QK_NOPE), F32)],
                         axis=-1).reshape(KV_LORA, MLA_HEADS * QK_PAD)
    wv = kvb[..., QK_NOPE:].reshape(KV_LORA, MLA_HEADS * V_HEAD).T
    eye = jnp.eye(QK_ROPE, dtype=F32)
    place = lambda blk: _pad_rows(jnp.tile(jnp.concatenate(
        [jnp.zeros((QK_ROPE, QK_NOPE), F32), blk, jnp.zeros((QK_ROPE, QK_PAD - QK_NOPE - QK_ROPE), F32)],
        axis=1), (1, MLA_HEADS)), KVR_PAD - KV_LORA)
    e_mat = place(eye)
    er_mat = place(_rot_cols(eye))
    return tuple(m.astype(BF16) for m in (wq, wqr, wk, wv, e_mat, er_mat))


def kernel(x, positions, ln0_g, ln0_b, w_in, rw_mu, rw_w0, rw_w2, rw_a0, rw_a2, rw_g2, rw_kk, rw_ka, rw_rk, rw_lnx_g, rw_lnx_b, rw_v0, rw_v1, rw_v2, mla_q_norm, mla_q_b, mla_kv_norm, mla_kv_b, conv_dw, conv_db, conv_ln_g, conv_ln_b, w_branch, w_out, ln1_g, ln1_b, router_w, router_b, moe_w13, moe_w2, ln2_g, ln2_b):
    batch, seq, d = x.shape
    t = batch * seq
    h, hb = layer_norm0(x.reshape(t, d), ln0_g, ln0_b)
    cos_t, sin_t = _rope_tables(positions)
    rw_t = router_w.T
    rw_hi = rw_t.astype(BF16)
    rw_lo = (rw_t - rw_hi.astype(F32)).astype(BF16)
    o_q = RW_SHIFT
    o_kv = o_q + Q_LORA
    o_conv = o_kv + KV_LORA + QK_ROPE
    o_gate = o_conv + 2 * CONV_W
    v_first = None
    for l in range(DEPTH):
        wi = w_in[l]
        w_rw = _pad_cols(wi[:, :RW_SHIFT], RW_PAD).astype(BF16)
        w_q = wi[:, o_q:o_kv].astype(BF16)
        w_kvr = _pad_cols(wi[:, o_kv:o_conv], KVR_PAD).astype(BF16)
        w_conv = wi[:, o_conv:o_gate].astype(BF16)
        w_gate = wi[:, o_gate:].astype(BF16)

        z_rw = matmul(hb, w_rw, 512, RW_PAD // 3)
        mu = _pad_cols(rw_mu[l].reshape(1, RW_SHIFT), RW_PAD)
        vecs = _pad_rows(jnp.stack([rw_w0[l], rw_a0[l], rw_kk[l], rw_ka[l], rw_rk[l]]), SUBLANES)
        zw2 = jnp.zeros((DECAY_LORA, RWKV_W), F32)
        w_wa = jnp.concatenate([jnp.concatenate([rw_w2[l], zw2], axis=1),
                                jnp.concatenate([zw2, rw_a2[l]], axis=1)], axis=0)
        w_g = _pad_rows(rw_g2[l], LORA_PAD - LANES)
        lora_w = []
        for wmat in (w_wa, w_g):
            w_hi = wmat.astype(BF16)
            lora_w += [w_hi, (wmat - w_hi.astype(F32)).astype(BF16)]
        if l == 0:
            vres = None
        else:
            vres = (v_first, rw_v0[l - 1].reshape(1, RWKV_W),
                    _pad_cols(rw_v1[l - 1], LANES).astype(BF16), _pad_rows(rw_v2[l - 1], LANES).astype(BF16))
        prep = rwkv_prep(z_rw, batch, seq, mu, vecs, lora_w, vres)
        if l == 0:
            v_first = prep[6]
        y_a = rwkv_chunks(prep, batch, seq, rw_lnx_g[l], rw_lnx_b[l])

        z_q = matmul(hb, w_q, 512, Q_LORA)
        z_kvr = matmul(hb, w_kvr, 512, KVR_PAD)
        mw = _mla_weights(mla_q_b[l], mla_kv_b[l])
        q_all, k_all, v_all = mla_prep(z_q, z_kvr, cos_t, sin_t, mla_q_norm[l].reshape(1, Q_LORA),
                                       mla_kv_norm[l].reshape(1, KV_LORA), *mw)
        y_b = attention(q_all, k_all, v_all, batch, seq)

        z_conv = matmul(hb, w_conv, 512, CONV_W)
        y_c = conformer_conv(z_conv, batch, seq, _pad_rows(conv_dw[l], CONV_HALO), conv_db[l],
                             conv_ln_g[l], conv_ln_b[l])

        z_gate = [matmul(hb, w_gate[:, i * d:(i + 1) * d], 512, 1024) for i in range(N_BRANCH)]
        merged = branch_merge(y_a, y_b, y_c, z_gate, w_branch[l].astype(BF16))
        h, hb, e_rows, gate_rows, counts = out_ln_router(merged, w_out[l].astype(BF16), h, ln1_g[l], ln1_b[l],
                                                         rw_hi, rw_lo, router_b)

        tok_pad, block_e, n_valid, p0, p1, gate_w = _route(e_rows, gate_rows, counts)
        ys = moe_experts(hb[tok_pad], block_e, n_valid, moe_w13, moe_w2, l)
        final = l == DEPTH - 1
        res = combine_ln(h, ys[p0], ys[p1], gate_w, ln2_g[l], ln2_b[l], final)
        if final:
            h = res[0]
        else:
            h, hb = res
    return h.reshape(batch, seq, d)
```

```python
import functools

import jax
import jax.numpy as jnp
from jax import lax
from jax.experimental import pallas as pl
from jax.experimental.pallas import tpu as pltpu

F32 = jnp.float32
BF16 = jnp.bfloat16

D_MODEL = 2048
DEPTH = 4
RWKV_HEADS = 16
RWKV_HEAD = 64
RWKV_W = 1024
DECAY_LORA = 64
AAA_LORA = 64
MV_LORA = 32
GATE_LORA = 160
RWKV_GN_EPS = 64e-5
MLA_HEADS = 8
Q_LORA = 512
KV_LORA = 256
QK_NOPE = 128
QK_ROPE = 64
V_HEAD = 128
ROPE_THETA = 10000.0
ATTN_SCALE = (QK_NOPE + QK_ROPE) ** -0.5
CONV_W = 1024
CONV_K = 31
N_BRANCH = 3
N_EXPERTS = 32
N_GROUPS = 4
EXP_PER_GROUP = 8
TOP_K = 2
D_EXPERT = 512
DN_ALPHA = (2 * DEPTH) ** 0.25
LN_EPS = 1e-5
RMS_EPS = 1e-6
RW_SHIFT = 3 * RWKV_W + DECAY_LORA + AAA_LORA + GATE_LORA
LORA_W = DECAY_LORA + AAA_LORA + GATE_LORA

LANES = 128
SUBLANES = 8
VMEM_LIMIT = 48 * 1024 * 1024

LORA_PAD = 384
RW_PAD = 3 * RWKV_W + LORA_PAD
KVR_PAD = 384
QK_PAD = 256

BF16_ROWS = 16
ATTN_TILE = 512
CHUNK = 64
GROUP_W = 256
MOE_TB = 256
NEG_BIG = -1e30


def _cparams(sem):
    return pltpu.CompilerParams(dimension_semantics=sem, vmem_limit_bytes=VMEM_LIMIT)


def _dot(a, b):
    return jnp.dot(a, b, preferred_element_type=F32)


def _dot_nt(a, b):
    return lax.dot_general(a, b, (((1,), (1,)), ((), ())), preferred_element_type=F32)


def _dot_tn(a, b):
    return lax.dot_general(a, b, (((0,), (0,)), ((), ())), preferred_element_type=F32)


def _split2(x):
    hi = x.astype(BF16)
    lo = (x - hi.astype(F32)).astype(BF16)
    return hi, lo


def _split3(x):
    hi = x.astype(BF16)
    r1 = x - hi.astype(F32)
    mid = r1.astype(BF16)
    lo = (r1 - mid.astype(F32)).astype(BF16)
    return hi, mid, lo


def _dot_x2(x, w_bf16):
    hi, lo = _split2(x)
    return _dot(hi, w_bf16) + _dot(lo, w_bf16)


def _sigmoid(x):
    return 1.0 / (1.0 + jnp.exp(-x))


def _mm_kernel(a_ref, b_ref, o_ref):
    o_ref[...] = _dot(a_ref[...], b_ref[...]).astype(o_ref.dtype)


def matmul(a, b, tm, tn, out_dtype=BF16):
    m, k = a.shape
    n = b.shape[1]
    return pl.pallas_call(
        _mm_kernel,
        grid=(n // tn, m // tm),
        in_specs=[pl.BlockSpec((tm, k), lambda j, i: (i, 0)),
                  pl.BlockSpec((k, tn), lambda j, i: (0, j))],
        out_specs=pl.BlockSpec((tm, tn), lambda j, i: (i, j)),
        out_shape=jax.ShapeDtypeStruct((m, n), out_dtype),
        compiler_params=_cparams(("parallel", "parallel")),
        name="matmul",
    )(a, b)


def _ln_rows(x, g, b):
    mu = jnp.mean(x, axis=-1, keepdims=True)
    xc = x - mu
    var = jnp.mean(xc * xc, axis=-1, keepdims=True)
    return xc * lax.rsqrt(var + LN_EPS) * g + b


def _ln0_kernel(x_ref, g_ref, b_ref, h_ref, hb_ref):
    h = _ln_rows(x_ref[...], g_ref[...], b_ref[...])
    h_ref[...] = h
    hb_ref[...] = h.astype(BF16)


def layer_norm0(x2, g, b, tm=256):
    t, d = x2.shape
    row = pl.BlockSpec((tm, d), lambda i: (i, 0))
    vec = pl.BlockSpec((1, d), lambda i: (0, 0))
    return pl.pallas_call(
        _ln0_kernel, grid=(t // tm,), in_specs=[row, vec, vec], out_specs=[row, row],
        out_shape=[jax.ShapeDtypeStruct((t, d), F32), jax.ShapeDtypeStruct((t, d), BF16)],
        compiler_params=_cparams(("parallel",)), name="ln0",
    )(x2, g.reshape(1, d), b.reshape(1, d))


def _rwkv_prep_kernel(has_vres, *refs):
    if has_vres:
        (z_ref, zp_ref, mu_ref, vec_ref, wa_hi_ref, wa_lo_ref, wg_hi_ref, wg_lo_ref, bd_ref, tri_ref, ones_ref,
         vf_ref, v0_ref, v1_ref, v2_ref,
         rt_ref, kap_ref, bt_ref, kt_ref, bh_ref, kh_ref, v_ref, bonus_ref, g_ref, pl8_ref) = refs
    else:
        (z_ref, zp_ref, mu_ref, vec_ref, wa_hi_ref, wa_lo_ref, wg_hi_ref, wg_lo_ref, bd_ref, tri_ref, ones_ref,
         rt_ref, kap_ref, bt_ref, kt_ref, bh_ref, kh_ref, v_ref, bonus_ref, g_ref, pl8_ref) = refs
    tm = z_ref.shape[0]
    first = pl.program_id(1) == 0

    z = z_ref[...].astype(F32)
    prev = zp_ref[SUBLANES - 1:SUBLANES, :].astype(F32)
    prev = jnp.where(first, 0.0, prev)
    row = lax.broadcasted_iota(jnp.int32, z.shape, 0)
    zs = jnp.where(row == 0, prev, pltpu.roll(z, 1, 0))
    z = z + (zs - z) * mu_ref[...]

    r = z[:, 0:RWKV_W]
    k = z[:, RWKV_W:2 * RWKV_W]
    v = z[:, 2 * RWKV_W:3 * RWKV_W]
    zl = z[:, 3 * RWKV_W:RW_PAD]
    col = lax.broadcasted_iota(jnp.int32, zl.shape, 1)
    act = jnp.where(col < DECAY_LORA, jnp.tanh(zl),
                    jnp.where(col < DECAY_LORA + AAA_LORA, zl, _sigmoid(zl)))

    def lora_dot(a, hi_ref, lo_ref):
        a_hi, a_lo = _split2(a)
        hi = hi_ref[...]
        return _dot(a_hi, hi) + _dot(a_lo, hi) + _dot(a_hi, lo_ref[...])

    lora_wa = lora_dot(act[:, :LANES], wa_hi_ref, wa_lo_ref)
    g = lora_dot(act[:, LANES:], wg_hi_ref, wg_lo_ref)
    w0 = vec_ref[0:1, :]
    a0 = vec_ref[1:2, :]
    k_k = vec_ref[2:3, :]
    k_a = vec_ref[3:4, :]
    r_k = vec_ref[4:5, :]
    x = -(w0 + lora_wa[:, 0:RWKV_W])
    softplus = jnp.maximum(x, 0.0) + jnp.log(1.0 + jnp.exp(-jnp.abs(x)))
    logd = -jnp.exp(-softplus - 0.5)
    a = _sigmoid(a0 + lora_wa[:, RWKV_W:])

    if has_vres:
        lo_rank = _dot(_dot(v.astype(BF16), v1_ref[...]).astype(BF16), v2_ref[...])
        v = v + (vf_ref[...].astype(F32) - v) * _sigmoid(v0_ref[...] + lo_rank)

    bd = bd_ref[...]

    def seg_sum(t):
        return jnp.concatenate(
            [_dot_x2(t[:, c:c + GROUP_W], bd) for c in range(0, RWKV_W, GROUP_W)], axis=1)

    kk = k * k_k
    kk = kk * lax.rsqrt(jnp.maximum(seg_sum(kk * kk), 1e-24))
    kmod = k * (1.0 + (a - 1.0) * k_a)
    b = kk * a
    bonus = seg_sum(r * kmod * r_k) * v

    l_hi, l_mid, l_lo = _split3(logd)
    tri = tri_ref[...]
    ones = ones_ref[...]
    c = _dot(tri, l_hi) + _dot(tri, l_mid) + _dot(tri, l_lo)
    cl = _dot(ones, l_hi) + _dot(ones, l_mid) + _dot(ones, l_lo)
    e_c = jnp.exp(c)
    e_prev = jnp.exp(c - logd)
    e_inv = jnp.exp(-c)
    e_rest = jnp.exp(cl - c)
    rt_ref[...] = (r * e_c).astype(BF16)
    kap_ref[...] = (kk * e_prev).astype(BF16)
    bt_ref[...] = (b * e_inv).astype(BF16)
    kt_ref[...] = (kmod * e_inv).astype(BF16)
    bh_ref[...] = (b * e_rest).astype(BF16)
    kh_ref[...] = (kmod * e_rest).astype(BF16)
    v_ref[...] = v.astype(BF16)
    bonus_ref[...] = bonus.astype(BF16)
    g_ref[...] = g.astype(BF16)
    e_cl = jnp.exp(cl)
    pl8_ref[...] = jnp.concatenate(
        [e_cl[j * CHUNK:j * CHUNK + SUBLANES] for j in range(tm // CHUNK)], axis=0)


def rwkv_prep(z_rw, batch, seq, mu, vecs, lora_w, vres, tm=256):
    t = z_rw.shape[0]
    nb = seq // tm
    has_vres = vres is not None
    ii = lax.broadcasted_iota(jnp.int32, (GROUP_W, GROUP_W), 0)
    jj = lax.broadcasted_iota(jnp.int32, (GROUP_W, GROUP_W), 1)
    bd = (ii // RWKV_HEAD == jj // RWKV_HEAD).astype(BF16)
    it = lax.broadcasted_iota(jnp.int32, (tm, tm), 0)
    jt = lax.broadcasted_iota(jnp.int32, (tm, tm), 1)
    same = it // CHUNK == jt // CHUNK
    tri = (same & (jt <= it)).astype(BF16)
    ones = same.astype(BF16)

    zrow = pl.BlockSpec((tm, RW_PAD), lambda b, i: (b * nb + i, 0))
    zprev = pl.BlockSpec(
        (SUBLANES, RW_PAD), lambda b, i: (jnp.maximum((b * nb + i) * (tm // SUBLANES) - 1, 0), 0))
    full = lambda shp: pl.BlockSpec(shp, lambda b, i: (0,) * len(shp))
    wrow = pl.BlockSpec((tm, RWKV_W), lambda b, i: (b * nb + i, 0))
    in_specs = [zrow, zprev, full((1, RW_PAD)), full((SUBLANES, RWKV_W)),
                full((LANES, 2 * RWKV_W)), full((LANES, 2 * RWKV_W)),
                full((LORA_PAD - LANES, RWKV_W)), full((LORA_PAD - LANES, RWKV_W)),
                full((GROUP_W, GROUP_W)), full((tm, tm)), full((tm, tm))]
    args = [z_rw, z_rw, mu, vecs, *lora_w, bd, tri, ones]
    if has_vres:
        v_first, v0, v1, v2 = vres
        in_specs += [wrow, full((1, RWKV_W)), full((RWKV_W, LANES)), full((LANES, RWKV_W))]
        args += [v_first, v0, v1, v2]
    p8 = tm // CHUNK * SUBLANES
    out_specs = [wrow] * 9 + [pl.BlockSpec((p8, RWKV_W), lambda b, i: (b * nb + i, 0))]
    out_shape = [jax.ShapeDtypeStruct((t, RWKV_W), BF16)] * 9 + [
        jax.ShapeDtypeStruct((t // CHUNK * SUBLANES, RWKV_W), F32)]
    return pl.pallas_call(
        functools.partial(_rwkv_prep_kernel, has_vres),
        grid=(batch, nb), in_specs=in_specs, out_specs=out_specs, out_shape=out_shape,
        compiler_params=_cparams(("parallel", "parallel")), name="rwkv_prep",
    )(*args)


def _rwkv_chunk_kernel(rt_ref, kap_ref, bt_ref, kt_ref, bh_ref, kh_ref, v_ref, pl8_ref,
                       bonus_ref, g_ref, lng_ref, lnb_ref, y_ref, h_ref):
    nbat, tb = rt_ref.shape[0], rt_ref.shape[1]
    w = GROUP_W

    @pl.when(pl.program_id(1) == 0)
    def _():
        h_ref[...] = jnp.zeros_like(h_ref)

    ri = lax.broadcasted_iota(jnp.int32, (w, w), 0)
    ci = lax.broadcasted_iota(jnp.int32, (w, w), 1)
    bd_mask = (ri >> 6) == (ci >> 6)
    eye = ri == ci
    tr = lax.broadcasted_iota(jnp.int32, (CHUNK, w), 0)
    sc = lax.broadcasted_iota(jnp.int32, (CHUNK, w), 1) & (CHUNK - 1)
    strict = sc < tr
    incl = sc <= tr
    seg_mean = jnp.where(bd_mask, 1.0 / RWKV_HEAD, 0.0).astype(BF16)

    def bd(y):
        return jnp.where(bd_mask, jnp.concatenate([y] * (w // CHUNK), axis=0), jnp.zeros((), y.dtype))

    def mm(x, y_bd):
        return _dot(x.astype(BF16), y_bd)

    nch = tb // CHUNK
    cs = range(nbat * nch)
    rows = [(c // nch, slice((c % nch) * CHUNK, (c % nch + 1) * CHUNK)) for c in cs]
    rt = [rt_ref[b, r, :] for b, r in rows]
    kap = [kap_ref[b, r, :] for b, r in rows]
    v = [v_ref[b, r, :] for b, r in rows]
    bh = [bh_ref[b, r, :] for b, r in rows]
    bt_in = [bt_ref[b, r, :] for b, r in rows]
    kt_in = [kt_ref[b, r, :] for b, r in rows]
    kh = [kh_ref[b, r, :] for b, r in rows]
    pl_row = [pl8_ref[c // nch, (c % nch) * SUBLANES:(c % nch) * SUBLANES + 1, :] for c in cs]
    lhs = [jnp.concatenate([kap[c], rt[c]], axis=0) for c in cs]
    a_b = [_dot_nt(lhs[c], bd(bt_in[c])) for c in cs]
    a_k = [_dot_nt(lhs[c], bd(kt_in[c])) for c in cs]
    a_rb = [jnp.where(incl, a_b[c][CHUNK:], 0.0).astype(BF16) for c in cs]
    a_kk = [jnp.concatenate([jnp.where(strict, a_k[c][:CHUNK], 0.0),
                             jnp.where(incl, a_k[c][CHUNK:], 0.0)], axis=0) for c in cs]

    n_pow = [-jnp.where(strict, a_b[c][:CHUNK], 0.0) for c in cs]
    t_inv = [jnp.where(sc == tr, 1.0, 0.0) + n_pow[c] for c in cs]
    n_pow = [mm(n_pow[c], bd(n_pow[c].astype(BF16))) for c in cs]
    for _ in range(4):
        both = [mm(jnp.concatenate([n_pow[c], t_inv[c]], axis=0), bd(n_pow[c].astype(BF16))) for c in cs]
        n_pow = [both[c][:CHUNK] for c in cs]
        t_inv = [t_inv[c] + both[c][CHUNK:] for c in cs]
    t_inv = [t_inv[c] + mm(t_inv[c], bd(n_pow[c].astype(BF16))) for c in cs]

    av = [mm(a_kk[c], bd(v[c])) for c in cs]
    t_bf = [t_inv[c].astype(BF16) for c in cs]
    tk = [_dot(t_bf[c], jnp.concatenate([bd(kap[c]), bd(av[c][:CHUNK].astype(BF16))], axis=1)).astype(BF16)
          for c in cs]
    ar = [_dot(a_rb[c], jnp.concatenate([bd(tk[c][:, :w]), bd(tk[c][:, w:])], axis=1)) for c in cs]
    r_hat = [(rt[c].astype(F32) - ar[c][:, :w]).astype(BF16) for c in cs]
    y_in = [av[c][CHUNK:] - ar[c][:, w:] for c in cs]
    lhs2 = [jnp.concatenate([r_hat[c], tk[c][:, :w]], axis=0) for c in cs]
    s_t = [jnp.concatenate([kh[c], bh[c]], axis=0).astype(F32).T.astype(BF16) for c in cs]
    pl_col = [jnp.sum(jnp.where(eye, pl_row[c], 0.0), axis=1, keepdims=True) for c in cs]

    h = [h_ref[b] for b in range(nbat)]
    y = [None] * len(cs)
    for ch in range(nch):
        for b in range(nbat):
            c = b * nch + ch
            h_hi, h_lo = _split2(h[b])
            uy = _dot(lhs2[c], h_hi) + _dot(lhs2[c], h_lo)
            y[c] = y_in[c] + uy[:CHUNK]
            r_mat = jnp.concatenate([v[c], (-(tk[c][:, w:].astype(F32) + uy[CHUNK:])).astype(BF16)], axis=0)
            h[b] = h[b] * pl_col[c] + jnp.where(bd_mask, _dot(s_t[c], r_mat), 0.0)
    for b in range(nbat):
        h_ref[b] = h[b]

    y_all = jnp.concatenate(y, axis=0)
    mu = _dot_x2(y_all, seg_mean)
    yc = y_all - mu
    var = _dot_x2(yc * yc, seg_mean)
    out = yc * lax.rsqrt(var + RWKV_GN_EPS) * lng_ref[...] + lnb_ref[...]
    for b in range(nbat):
        o = (out[b * tb:(b + 1) * tb] + bonus_ref[b].astype(F32)) * g_ref[b].astype(F32)
        y_ref[b] = o.astype(y_ref.dtype)


def rwkv_chunks(prep, batch, seq, lnx_g, lnx_b, tb=256):
    rt, kap, bt, kt, bh, kh, v, bonus, g, pl8 = prep
    nb = seq // tb
    ng = RWKV_W // GROUP_W
    p8_rows = tb // CHUNK * SUBLANES
    seq3 = lambda a: a.reshape(batch, a.shape[0] // batch, RWKV_W)
    blk = pl.BlockSpec((batch, tb, GROUP_W), lambda q, i: (0, i, q))
    p8 = pl.BlockSpec((batch, p8_rows, GROUP_W), lambda q, i: (0, i, q))
    vec = pl.BlockSpec((1, GROUP_W), lambda q, i: (0, q))
    y = pl.pallas_call(
        _rwkv_chunk_kernel,
        grid=(ng, nb),
        in_specs=[blk] * 7 + [p8, blk, blk, vec, vec],
        out_specs=blk,
        out_shape=jax.ShapeDtypeStruct((batch, seq, RWKV_W), BF16),
        scratch_shapes=[pltpu.VMEM((batch, GROUP_W, GROUP_W), F32)],
        compiler_params=_cparams(("parallel", "arbitrary")), name="rwkv_chunks",
    )(*[seq3(a) for a in (rt, kap, bt, kt, bh, kh, v, pl8, bonus, g)],
      lnx_g.reshape(1, RWKV_W), lnx_b.reshape(1, RWKV_W))
    return y.reshape(batch * seq, RWKV_W)


def _mla_prep_kernel(zq_ref, zkv_ref, cos_ref, sin_ref, qn_ref, kvn_ref, wq_ref, wqr_ref,
                     wk_ref, wv_ref, e_ref, er_ref, q_ref, k_ref, v_ref):
    zq = zq_ref[...].astype(F32)
    qn = (zq * lax.rsqrt(jnp.mean(zq * zq, axis=-1, keepdims=True) + RMS_EPS) * qn_ref[...]).astype(BF16)
    zkv = zkv_ref[:, 0:KV_LORA].astype(F32)
    cn = (zkv * lax.rsqrt(jnp.mean(zkv * zkv, axis=-1, keepdims=True) + RMS_EPS) * kvn_ref[...]).astype(BF16)
    kr = zkv_ref[:, KV_LORA:KVR_PAD]
    cos = cos_ref[...]
    sin = sin_ref[...]
    q_all = _dot(qn, wq_ref[...])
    q_rot = _dot(qn, wqr_ref[...])
    k_all = _dot(cn, wk_ref[...]) + _dot(kr, e_ref[...])
    k_rot = _dot(kr, er_ref[...])
    for h in range(MLA_HEADS):
        cols = slice(h * QK_PAD, (h + 1) * QK_PAD)
        q_ref[:, cols] = ((q_all[:, cols] * cos + q_rot[:, cols] * sin) * ATTN_SCALE).astype(BF16)
        k_ref[:, cols] = (k_all[:, cols] * cos + k_rot[:, cols] * sin).astype(BF16)
    v_ref[...] = _dot_nt(wv_ref[...], cn).astype(BF16)


def mla_prep(z_q, z_kvr, cos_t, sin_t, qn, kvn, wq, wqr, wk, wv_t, e_mat, er_mat, tm=ATTN_TILE):
    t = z_q.shape[0]
    hw = MLA_HEADS * QK_PAD
    vw = MLA_HEADS * V_HEAD
    row = lambda w: pl.BlockSpec((tm, w), lambda i: (i, 0))
    full = lambda a: pl.BlockSpec(a.shape, lambda i: (0, 0))
    return pl.pallas_call(
        _mla_prep_kernel, grid=(t // tm,),
        in_specs=[row(Q_LORA), row(KVR_PAD), row(QK_PAD), row(QK_PAD), full(qn), full(kvn),
                  full(wq), full(wqr), full(wk), full(wv_t), full(e_mat), full(er_mat)],
        out_specs=[row(hw), row(hw), pl.BlockSpec((None, vw, tm), lambda i: (i, 0, 0))],
        out_shape=[jax.ShapeDtypeStruct((t, hw), BF16), jax.ShapeDtypeStruct((t, hw), BF16),
                   jax.ShapeDtypeStruct((t // tm, vw, tm), BF16)],
        compiler_params=_cparams(("parallel",)), name="mla_prep",
    )(z_q, z_kvr, cos_t, sin_t, qn, kvn, wq, wqr, wk, wv_t, e_mat, er_mat)


def _attn_kernel(q_ref, k_ref, vt_ref, o_ref, st0_ref, st1_ref, mx0_ref, mx1_ref, m_ref, acc_ref):
    tq = q_ref.shape[0]
    tk = tq
    i = pl.program_id(2)
    q = q_ref[...]
    ones = jnp.ones((BF16_ROWS, tk), BF16)

    def qk(j, st_ref, mx_ref):
        start = pl.multiple_of(j * tk, tk)
        st = _dot_nt(k_ref[pl.ds(start, tk), :], q)
        st_ref[...] = st
        mx_ref[...] = jnp.max(st, axis=0, keepdims=True)

    def consume(j, st_ref, mx_ref, masked):
        st = st_ref[...]
        if masked:
            ki = lax.broadcasted_iota(jnp.int32, st.shape, 0)
            qi = lax.broadcasted_iota(jnp.int32, st.shape, 1)
            st = jnp.where(ki <= qi, st, NEG_BIG)
            mx = jnp.max(st, axis=0, keepdims=True)
        else:
            mx = mx_ref[...]
        m = m_ref[...]
        m_new = jnp.maximum(m, mx)
        alpha = jnp.exp(m - m_new)
        p = jnp.exp(st - m_new).astype(BF16)
        vt = jnp.concatenate([vt_ref[j], ones], axis=0)
        acc_ref[...] = alpha * acc_ref[...] + _dot(vt, p)
        m_ref[...] = m_new

    m_ref[...] = jnp.full(m_ref.shape, NEG_BIG, F32)
    acc_ref[...] = jnp.zeros(acc_ref.shape, F32)
    qk(0, st0_ref, mx0_ref)

    def pair(jj, carry):
        j = 2 * jj
        qk(j + 1, st1_ref, mx1_ref)
        consume(j, st0_ref, mx0_ref, False)
        qk(j + 2, st0_ref, mx0_ref)
        consume(j + 1, st1_ref, mx1_ref, False)
        return carry

    lax.fori_loop(0, i // 2, pair, 0)

    @pl.when(i % 2 == 0)
    def _():
        consume(i, st0_ref, mx0_ref, True)

    @pl.when(i % 2 == 1)
    def _():
        qk(i, st1_ref, mx1_ref)
        consume(i - 1, st0_ref, mx0_ref, False)
        consume(i, st1_ref, mx1_ref, True)

    acc = acc_ref[...]
    out = acc[:V_HEAD] / acc[V_HEAD:V_HEAD + 1]
    o_ref[...] = out.T.astype(o_ref.dtype)


def attention(q_all, k_all, vt_all, batch, seq, tq=ATTN_TILE):
    t = q_all.shape[0]
    nq = seq // tq
    return pl.pallas_call(
        _attn_kernel,
        grid=(batch, MLA_HEADS, nq),
        in_specs=[pl.BlockSpec((tq, QK_PAD), lambda b, h, i: (b * nq + i, h)),
                  pl.BlockSpec((seq, QK_PAD), lambda b, h, i: (b, h)),
                  pl.BlockSpec((nq, V_HEAD, tq), lambda b, h, i: (b, h, 0))],
        out_specs=pl.BlockSpec((tq, V_HEAD), lambda b, h, i: (b * nq + i, h)),
        out_shape=jax.ShapeDtypeStruct((t, MLA_HEADS * V_HEAD), BF16),
        scratch_shapes=[pltpu.VMEM((tq, tq), F32), pltpu.VMEM((tq, tq), F32),
                        pltpu.VMEM((1, tq), F32), pltpu.VMEM((1, tq), F32),
                        pltpu.VMEM((1, tq), F32), pltpu.VMEM((V_HEAD + BF16_ROWS, tq), F32)],
        compiler_params=_cparams(("parallel", "parallel", "arbitrary")), name="attention",
    )(q_all, k_all, vt_all)


CONV_HALO = 32


def _conv_kernel(z_ref, zh_ref, dw_ref, db_ref, g_ref, b_ref, o_ref, u_ref, s_ref):
    tm = z_ref.shape[0]
    first = pl.program_id(1) == 0

    def glu(zz):
        zz = zz.astype(F32)
        return zz[:, :CONV_W] * _sigmoid(zz[:, CONV_W:])

    u_ref[0:CONV_HALO, :] = jnp.where(first, 0.0, glu(zh_ref[...]))
    u_ref[CONV_HALO:, :] = glu(z_ref[...])
    off = CONV_HALO - (CONV_K - 1)
    acc = jnp.zeros((tm, CONV_W), F32) + db_ref[...]
    for res in range(SUBLANES):
        taps = [j for j in range(CONV_K) if (off + j) % SUBLANES == res]
        span = max(off + j for j in taps) - res + tm
        if res == 0:
            src_ref = u_ref
        else:
            s_ref[0:span, :] = u_ref[res:res + span, :]
            src_ref = s_ref
        for j in taps:
            lo = off + j - res
            acc = acc + dw_ref[j:j + 1, :] * src_ref[lo:lo + tm, :]
    y = _ln_rows(acc, g_ref[...], b_ref[...])
    o_ref[...] = (y * _sigmoid(y)).astype(o_ref.dtype)


def conformer_conv(z_conv, batch, seq, dw, db, ln_g, ln_b, tm=256):
    t = z_conv.shape[0]
    nb = seq // tm
    ratio = tm // CONV_HALO
    vec = pl.BlockSpec((1, CONV_W), lambda b, i: (0, 0))
    return pl.pallas_call(
        _conv_kernel, grid=(batch, nb),
        in_specs=[pl.BlockSpec((tm, 2 * CONV_W), lambda b, i: (b * nb + i, 0)),
                  pl.BlockSpec((CONV_HALO, 2 * CONV_W),
                               lambda b, i: (jnp.maximum((b * nb + i) * ratio - 1, 0), 0)),
                  pl.BlockSpec((CONV_HALO, CONV_W), lambda b, i: (0, 0)), vec, vec, vec],
        out_specs=pl.BlockSpec((tm, CONV_W), lambda b, i: (b * nb + i, 0)),
        out_shape=jax.ShapeDtypeStruct((t, CONV_W), BF16),
        scratch_shapes=[pltpu.VMEM((tm + CONV_HALO, CONV_W), F32), pltpu.VMEM((tm + CONV_HALO, CONV_W), F32)],
        compiler_params=_cparams(("parallel", "parallel")), name="conformer_conv",
    )(z_conv, z_conv, dw, db.reshape(1, CONV_W), ln_g.reshape(1, CONV_W), ln_b.reshape(1, CONV_W))


def _merge_kernel(ya_ref, yb_ref, yc_ref, ga_ref, gb_ref, gc_ref, wb_ref, o_ref):
    acc = None
    for i, (y_ref, zg_ref) in enumerate(((ya_ref, ga_ref), (yb_ref, gb_ref), (yc_ref, gc_ref))):
        term = _sigmoid(zg_ref[...].astype(F32)) * _dot(y_ref[...], wb_ref[i])
        acc = term if acc is None else acc + term
    o_ref[...] = acc.astype(o_ref.dtype)


def branch_merge(y_a, y_b, y_c, zg, wb, tm=512, tn=1024):
    t = y_a.shape[0]
    yspec = pl.BlockSpec((tm, RWKV_W), lambda j, i: (i, 0))
    gspec = pl.BlockSpec((tm, tn), lambda j, i: (i, j))
    return pl.pallas_call(
        _merge_kernel, grid=(D_MODEL // tn, t // tm),
        in_specs=[yspec, yspec, yspec, gspec, gspec, gspec,
                  pl.BlockSpec((N_BRANCH, RWKV_W, tn), lambda j, i: (0, 0, j))],
        out_specs=pl.BlockSpec((tm, tn), lambda j, i: (i, j)),
        out_shape=jax.ShapeDtypeStruct((t, D_MODEL), BF16),
        compiler_params=_cparams(("parallel", "parallel")), name="branch_merge",
    )(y_a, y_b, y_c, *zg, wb)


def _out_ln_router_kernel(m_ref, w_ref, h_ref, g_ref, b_ref, rw_hi_ref, rw_lo_ref, rb_ref, tri_ref,
                          ho_ref, hb_ref, ei_ref, gw_ref, cnt_out_ref, cnt_ref):
    x = DN_ALPHA * h_ref[...] + _dot(m_ref[...], w_ref[...])
    h = _ln_rows(x, g_ref[...], b_ref[...])
    ho_ref[...] = h
    hb_ref[...] = h.astype(BF16)
    h_hi, h_lo = _split2(h)
    rw_hi = rw_hi_ref[...]
    logits = _dot_nt(rw_hi, h_hi) + _dot_nt(rw_hi, h_lo) + _dot_nt(rw_lo_ref[...], h_hi)
    scores = _sigmoid(logits)
    sel = scores + rb_ref[...]
    tm = sel.shape[1]
    idx = lax.broadcasted_iota(jnp.int32, (EXP_PER_GROUP, tm), 0)

    def first_max(x):
        mx = jnp.max(x, axis=0, keepdims=True)
        return mx, jnp.min(jnp.where(x == mx, idx, EXP_PER_GROUP), axis=0, keepdims=True)

    best = None
    for grp in range(N_GROUPS):
        rows = slice(grp * EXP_PER_GROUP, (grp + 1) * EXP_PER_GROUP)
        x = sel[rows]
        sc_g = scores[rows]
        m1, i1 = first_max(x)
        m2, i2 = first_max(jnp.where(idx == i1, -jnp.inf, x))
        s1 = jnp.sum(jnp.where(idx == i1, sc_g, 0.0), axis=0, keepdims=True)
        s2 = jnp.sum(jnp.where(idx == i2, sc_g, 0.0), axis=0, keepdims=True)
        cand = (m1 + m2, i1 + grp * EXP_PER_GROUP, i2 + grp * EXP_PER_GROUP, s1, s2)
        if best is None:
            best = cand
        else:
            take = cand[0] > best[0]
            best = tuple(jnp.where(take, cn, bs) for cn, bs in zip(cand, best))
    _, e1, e2, s1, s2 = best
    @pl.when(pl.program_id(0) == 0)
    def _():
        cnt_ref[...] = jnp.zeros_like(cnt_ref)

    erow = lax.broadcasted_iota(jnp.int32, (N_EXPERTS, tm), 0)
    hit1 = erow == e1
    hit2 = erow == e2
    onehot = jnp.where(hit1, 1.0, jnp.where(hit2, 1.0, 0.0))
    before = cnt_ref[:, 0:1] + _dot(onehot.astype(BF16), tri_ref[...]) - onehot
    rank1 = jnp.sum(jnp.where(hit1, before, 0.0), axis=0, keepdims=True).astype(jnp.int32)
    rank2 = jnp.sum(jnp.where(hit2, before, 0.0), axis=0, keepdims=True).astype(jnp.int32)
    cnt_ref[...] = cnt_ref[...] + jnp.sum(onehot, axis=1, keepdims=True)
    cnt_out_ref[...] = cnt_ref[...]
    row = lax.broadcasted_iota(jnp.int32, (SUBLANES, tm), 0)
    ei_ref[...] = jnp.where(row == 0, e1, jnp.where(row == 1, e2,
                                                    jnp.where(row == 2, rank1, jnp.where(row == 3, rank2, 0))))
    inv = 1.0 / (s1 + s2)
    gw_ref[...] = jnp.where(row == 0, s1 * inv, jnp.where(row == 1, s2 * inv, 0.0))


def out_ln_router(merged, w_out, h, g, b, rw_hi, rw_lo, rb, tm=256):
    t, d = h.shape
    si = lax.broadcasted_iota(jnp.int32, (tm, tm), 0)
    ti = lax.broadcasted_iota(jnp.int32, (tm, tm), 1)
    tri = (si <= ti).astype(BF16)
    row = pl.BlockSpec((tm, d), lambda i: (i, 0))
    vec = pl.BlockSpec((1, d), lambda i: (0, 0))
    rws = pl.BlockSpec((N_EXPERTS, d), lambda i: (0, 0))
    tok = pl.BlockSpec((SUBLANES, tm), lambda i: (0, i))
    return pl.pallas_call(
        _out_ln_router_kernel, grid=(t // tm,),
        in_specs=[row, pl.BlockSpec((d, d), lambda i: (0, 0)), row, vec, vec, rws, rws,
                  pl.BlockSpec((N_EXPERTS, tm), lambda i: (0, 0)), pl.BlockSpec((tm, tm), lambda i: (0, 0))],
        out_specs=[row, row, tok, tok, pl.BlockSpec((N_EXPERTS, LANES), lambda i: (0, 0))],
        out_shape=[jax.ShapeDtypeStruct((t, d), F32), jax.ShapeDtypeStruct((t, d), BF16),
                   jax.ShapeDtypeStruct((SUBLANES, t), jnp.int32), jax.ShapeDtypeStruct((SUBLANES, t), F32),
                   jax.ShapeDtypeStruct((N_EXPERTS, LANES), F32)],
        scratch_shapes=[pltpu.VMEM((N_EXPERTS, LANES), F32)],
        compiler_params=_cparams(("arbitrary",)), name="out_ln_router",
    )(merged, w_out, h, g.reshape(1, d), b.reshape(1, d), rw_hi, rw_lo,
      jnp.broadcast_to(rb.reshape(N_EXPERTS, 1), (N_EXPERTS, tm)), tri)


def _moe_kernel(be_ref, nv_ref, x_ref, w13_ref, w2_ref, o_ref, w13b_ref, w2b_ref):
    i = pl.program_id(0)

    @pl.when((i == 0) | (be_ref[i] != be_ref[jnp.maximum(i - 1, 0)]))
    def _():
        w13b_ref[...] = w13_ref[...].astype(BF16)
        w2b_ref[...] = w2_ref[...].astype(BF16)

    @pl.when(i < nv_ref[0])
    def _():
        gu = _dot(x_ref[...], w13b_ref[...])
        gate = gu[:, :D_EXPERT]
        act = gate * _sigmoid(gate) * gu[:, D_EXPERT:]
        o_ref[...] = _dot(act.astype(BF16), w2b_ref[...]).astype(o_ref.dtype)

    @pl.when(i >= nv_ref[0])
    def _():
        o_ref[...] = jnp.zeros_like(o_ref)


def moe_experts(x_sorted, block_e, n_valid, w13, w2, layer):
    p, d = x_sorted.shape
    nblk = p // MOE_TB
    grid_spec = pltpu.PrefetchScalarGridSpec(
        num_scalar_prefetch=2, grid=(nblk,),
        in_specs=[pl.BlockSpec((MOE_TB, d), lambda i, be, nv: (i, 0)),
                  pl.BlockSpec((None, None, d, 2 * D_EXPERT), lambda i, be, nv: (layer, be[i], 0, 0)),
                  pl.BlockSpec((None, None, D_EXPERT, d), lambda i, be, nv: (layer, be[i], 0, 0))],
        out_specs=pl.BlockSpec((MOE_TB, d), lambda i, be, nv: (i, 0)),
        scratch_shapes=[pltpu.VMEM((d, 2 * D_EXPERT), BF16), pltpu.VMEM((D_EXPERT, d), BF16)])
    return pl.pallas_call(
        _moe_kernel, grid_spec=grid_spec,
        out_shape=jax.ShapeDtypeStruct((p, d), BF16),
        compiler_params=_cparams(("arbitrary",)), name="moe_experts",
    )(block_e, n_valid, x_sorted, w13, w2)


def _combine_ln_kernel(final, h_ref, y0_ref, y1_ref, gw_ref, g_ref, b_ref, *outs):
    gw = gw_ref[...]
    f = gw[:, 0:1] * y0_ref[...].astype(F32) + gw[:, 1:2] * y1_ref[...].astype(F32)
    h = _ln_rows(DN_ALPHA * h_ref[...] + f, g_ref[...], b_ref[...])
    outs[0][...] = h
    if not final:
        outs[1][...] = h.astype(BF16)


def combine_ln(h, y0, y1, gate_w, g, b, final, tm=256):
    t, d = h.shape
    row = pl.BlockSpec((tm, d), lambda i: (i, 0))
    vec = pl.BlockSpec((1, d), lambda i: (0, 0))
    out_specs = [row] if final else [row, row]
    out_shape = [jax.ShapeDtypeStruct((t, d), F32)] + ([] if final else [jax.ShapeDtypeStruct((t, d), BF16)])
    return pl.pallas_call(
        functools.partial(_combine_ln_kernel, final), grid=(t // tm,),
        in_specs=[row, row, row, pl.BlockSpec((tm, LANES), lambda i: (i, 0)), vec, vec],
        out_specs=out_specs, out_shape=out_shape,
        compiler_params=_cparams(("parallel",)), name="combine_ln",
    )(h, y0, y1, gate_w, g.reshape(1, d), b.reshape(1, d))


def _route(e_rows, gate_rows, counts):
    e_idx = e_rows[:TOP_K].T
    rank = e_rows[TOP_K:2 * TOP_K].T.reshape(-1)
    gate = gate_rows[:TOP_K].T
    t = e_idx.shape[0]
    a = t * TOP_K
    flat_e = e_idx.reshape(a)
    counts = counts[:, 0].astype(jnp.int32)
    padded = (counts + MOE_TB - 1) // MOE_TB * MOE_TB
    ends = jnp.cumsum(padded)
    pstarts = ends - padded
    e_hot = flat_e[:, None] == jnp.arange(N_EXPERTS, dtype=jnp.int32)[None, :]
    dest = jnp.sum(jnp.where(e_hot, pstarts[None, :], 0), axis=1) + rank
    p = a + N_EXPERTS * MOE_TB
    nblk = p // MOE_TB
    flat_tok = jnp.repeat(jnp.arange(t, dtype=jnp.int32), TOP_K)
    tok_pad = jnp.zeros((p,), jnp.int32).at[dest].set(flat_tok)
    block_start = jnp.arange(nblk, dtype=jnp.int32) * MOE_TB
    block_e = jnp.minimum(jnp.sum((ends[None, :] <= block_start[:, None]).astype(jnp.int32), axis=1),
                          N_EXPERTS - 1)
    n_valid = (ends[-1] // MOE_TB).astype(jnp.int32).reshape(1)
    dest2 = dest.reshape(t, TOP_K)
    gate_w = jnp.pad(gate, ((0, 0), (0, LANES - TOP_K)))
    return tok_pad, block_e, n_valid, dest2[:, 0], dest2[:, 1], gate_w


def _pad_cols(w, n):
    return jnp.pad(w, ((0, 0), (0, n - w.shape[1])))


def _pad_rows(w, n):
    return jnp.pad(w, ((0, n - w.shape[0]), (0, 0)))


def _rope_tables(positions):
    inv = ROPE_THETA ** (-jnp.arange(0, QK_ROPE, 2, dtype=F32) / QK_ROPE)
    ang = positions.astype(F32).reshape(-1)[:, None] * inv
    cos, sin = jnp.cos(ang), jnp.sin(ang)
    t = ang.shape[0]
    ones = jnp.ones((t, QK_NOPE), F32)
    zeros = jnp.zeros((t, QK_PAD - QK_NOPE - QK_ROPE), F32)
    cos_t = jnp.concatenate([ones, cos, cos, zeros], axis=1)
    sin_t = jnp.concatenate([jnp.zeros((t, QK_NOPE), F32), sin, sin, zeros], axis=1)
    return cos_t, sin_t


def _rot_cols(w):
    half = QK_ROPE // 2
    return jnp.concatenate([-w[..., half:], w[..., :half]], axis=-1)


def _mla_weights(q_b, kv_b):
    qb = q_b.reshape(Q_LORA, MLA_HEADS, QK_NOPE + QK_ROPE)
    zpad = jnp.zeros((Q_LORA, MLA_HEADS, QK_PAD - QK_NOPE - QK_ROPE), F32)
    wq = jnp.concatenate([qb, zpad], axis=-1).reshape(Q_LORA, MLA_HEADS * QK_PAD)
    wqr = jnp.concatenate([jnp.zeros((Q_LORA, MLA_HEADS, QK_NOPE), F32), _rot_cols(qb[..., QK_NOPE:]), zpad],
                          axis=-1).reshape(Q_LORA, MLA_HEADS * QK_PAD)
    kvb = kv_b.reshape(KV_LORA, MLA_HEADS, QK_NOPE + V_HEAD)
    wk = jnp.concatenate([kvb[..., :QK_NOPE], jnp.zeros((KV_LORA, MLA_HEADS, QK_PAD - QK_NOPE), F32)],
                         axis=-1).reshape(KV_LORA, MLA_HEADS * QK_PAD)
    wv = kvb[..., QK_NOPE:].reshape(KV_LORA, MLA_HEADS * V_HEAD).T
    eye = jnp.eye(QK_ROPE, dtype=F32)
    place = lambda blk: _pad_rows(jnp.tile(jnp.concatenate(
        [jnp.zeros((QK_ROPE, QK_NOPE), F32), blk, jnp.zeros((QK_ROPE, QK_PAD - QK_NOPE - QK_ROPE), F32)],
        axis=1), (1, MLA_HEADS)), KVR_PAD - KV_LORA)
    e_mat = place(eye)
    er_mat = place(_rot_cols(eye))
    return tuple(m.astype(BF16) for m in (wq, wqr, wk, wv, e_mat, er_mat))


def kernel(x, positions, ln0_g, ln0_b, w_in, rw_mu, rw_w0, rw_w2, rw_a0, rw_a2, rw_g2, rw_kk, rw_ka, rw_rk, rw_lnx_g, rw_lnx_b, rw_v0, rw_v1, rw_v2, mla_q_norm, mla_q_b, mla_kv_norm, mla_kv_b, conv_dw, conv_db, conv_ln_g, conv_ln_b, w_branch, w_out, ln1_g, ln1_b, router_w, router_b, moe_w13, moe_w2, ln2_g, ln2_b):
    batch, seq, d = x.shape
    t = batch * seq
    h, hb = layer_norm0(x.reshape(t, d), ln0_g, ln0_b)
    cos_t, sin_t = _rope_tables(positions)
    rw_t = router_w.T
    rw_hi = rw_t.astype(BF16)
    rw_lo = (rw_t - rw_hi.astype(F32)).astype(BF16)
    o_q = RW_SHIFT
    o_kv = o_q + Q_LORA
    o_conv = o_kv + KV_LORA + QK_ROPE
    o_gate = o_conv + 2 * CONV_W
    v_first = None
    for l in range(DEPTH):
        wi = w_in[l]
        w_rw = _pad_cols(wi[:, :RW_SHIFT], RW_PAD).astype(BF16)
        w_q = wi[:, o_q:o_kv].astype(BF16)
        w_kvr = _pad_cols(wi[:, o_kv:o_conv], KVR_PAD).astype(BF16)
        w_conv = wi[:, o_conv:o_gate].astype(BF16)
        w_gate = wi[:, o_gate:].astype(BF16)

        z_rw = matmul(hb, w_rw, 512, RW_PAD // 3)
        mu = _pad_cols(rw_mu[l].reshape(1, RW_SHIFT), RW_PAD)
        vecs = _pad_rows(jnp.stack([rw_w0[l], rw_a0[l], rw_kk[l], rw_ka[l], rw_rk[l]]), SUBLANES)
        zw2 = jnp.zeros((DECAY_LORA, RWKV_W), F32)
        w_wa = jnp.concatenate([jnp.concatenate([rw_w2[l], zw2], axis=1),
                                jnp.concatenate([zw2, rw_a2[l]], axis=1)], axis=0)
        w_g = _pad_rows(rw_g2[l], LORA_PAD - LANES)
        lora_w = []
        for wmat in (w_wa, w_g):
            w_hi = wmat.astype(BF16)
            lora_w += [w_hi, (wmat - w_hi.astype(F32)).astype(BF16)]
        if l == 0:
            vres = None
        else:
            vres = (v_first, rw_v0[l - 1].reshape(1, RWKV_W),
                    _pad_cols(rw_v1[l - 1], LANES).astype(BF16), _pad_rows(rw_v2[l - 1], LANES).astype(BF16))
        prep = rwkv_prep(z_rw, batch, seq, mu, vecs, lora_w, vres)
        if l == 0:
            v_first = prep[6]
        y_a = rwkv_chunks(prep, batch, seq, rw_lnx_g[l], rw_lnx_b[l])

        z_q = matmul(hb, w_q, 512, Q_LORA)
        z_kvr = matmul(hb, w_kvr, 512, KVR_PAD)
        mw = _mla_weights(mla_q_b[l], mla_kv_b[l])
        q_all, k_all, v_all = mla_prep(z_q, z_kvr, cos_t, sin_t, mla_q_norm[l].reshape(1, Q_LORA),
                                       mla_kv_norm[l].reshape(1, KV_LORA), *mw)
        y_b = attention(q_all, k_all, v_all, batch, seq)

        z_conv = matmul(hb, w_conv, 512, CONV_W)
        y_c = conformer_conv(z_conv, batch, seq, _pad_rows(conv_dw[l], CONV_HALO), conv_db[l],
                             conv_ln_g[l], conv_ln_b[l])

        z_gate = [matmul(hb, w_gate[:, i * d:(i + 1) * d], 512, 1024) for i in range(N_BRANCH)]
        merged = branch_merge(y_a, y_b, y_c, z_gate, w_branch[l].astype(BF16))
        h, hb, e_rows, gate_rows, counts = out_ln_router(merged, w_out[l].astype(BF16), h, ln1_g[l], ln1_b[l],
                                                         rw_hi, rw_lo, router_b)

        tok_pad, block_e, n_valid, p0, p1, gate_w = _route(e_rows, gate_rows, counts)
        ys = moe_experts(hb[tok_pad], block_e, n_valid, moe_w13, moe_w2, l)
        final = l == DEPTH - 1
        res = combine_ln(h, ys[p0], ys[p1], gate_w, ln2_g[l], ln2_b[l], final)
        if final:
            h = res[0]
        else:
            h, hb = res
    return h.reshape(batch, seq, d)
```

```python
import functools

import jax
import jax.numpy as jnp
from jax import lax
from jax.experimental import pallas as pl
from jax.experimental.pallas import tpu as pltpu

F32 = jnp.float32
BF16 = jnp.bfloat16

D_MODEL = 2048
DEPTH = 4
RWKV_HEADS = 16
RWKV_HEAD = 64
RWKV_W = 1024
DECAY_LORA = 64
AAA_LORA = 64
MV_LORA = 32
GATE_LORA = 160
RWKV_GN_EPS = 64e-5
MLA_HEADS = 8
Q_LORA = 512
KV_LORA = 256
QK_NOPE = 128
QK_ROPE = 64
V_HEAD = 128
ROPE_THETA = 10000.0
ATTN_SCALE = (QK_NOPE + QK_ROPE) ** -0.5
CONV_W = 1024
CONV_K = 31
N_BRANCH = 3
N_EXPERTS = 32
N_GROUPS = 4
EXP_PER_GROUP = 8
TOP_K = 2
D_EXPERT = 512
DN_ALPHA = (2 * DEPTH) ** 0.25
LN_EPS = 1e-5
RMS_EPS = 1e-6
RW_SHIFT = 3 * RWKV_W + DECAY_LORA + AAA_LORA + GATE_LORA
LORA_W = DECAY_LORA + AAA_LORA + GATE_LORA

LANES = 128
SUBLANES = 8
VMEM_LIMIT = 48 * 1024 * 1024

LORA_PAD = 384
RW_PAD = 3 * RWKV_W + LORA_PAD
KVR_PAD = 384
QK_PAD = 256

BF16_ROWS = 16
ATTN_TILE = 512
CHUNK = 64
GROUP_W = 256
MOE_TB = 512
NEG_BIG = -1e30


def _cparams(sem):
    return pltpu.CompilerParams(dimension_semantics=sem, vmem_limit_bytes=VMEM_LIMIT)


def _dot(a, b):
    return jnp.dot(a, b, preferred_element_type=F32)


def _dot_nt(a, b):
    return lax.dot_general(a, b, (((1,), (1,)), ((), ())), preferred_element_type=F32)


def _dot_tn(a, b):
    return lax.dot_general(a, b, (((0,), (0,)), ((), ())), preferred_element_type=F32)


def _split2(x):
    hi = x.astype(BF16)
    lo = (x - hi.astype(F32)).astype(BF16)
    return hi, lo


def _split3(x):
    hi = x.astype(BF16)
    r1 = x - hi.astype(F32)
    mid = r1.astype(BF16)
    lo = (r1 - mid.astype(F32)).astype(BF16)
    return hi, mid, lo


def _dot_x2(x, w_bf16):
    hi, lo = _split2(x)
    return _dot(hi, w_bf16) + _dot(lo, w_bf16)


def _sigmoid(x):
    return 1.0 / (1.0 + jnp.exp(-x))


def _mm_kernel(a_ref, b_ref, o_ref):
    o_ref[...] = _dot(a_ref[...], b_ref[...]).astype(o_ref.dtype)


def matmul(a, b, tm, tn, out_dtype=BF16):
    m, k = a.shape
    n = b.shape[1]
    return pl.pallas_call(
        _mm_kernel,
        grid=(n // tn, m // tm),
        in_specs=[pl.BlockSpec((tm, k), lambda j, i: (i, 0)),
                  pl.BlockSpec((k, tn), lambda j, i: (0, j))],
        out_specs=pl.BlockSpec((tm, tn), lambda j, i: (i, j)),
        out_shape=jax.ShapeDtypeStruct((m, n), out_dtype),
        compiler_params=_cparams(("parallel", "parallel")),
        name="matmul",
    )(a, b)


def _ln_rows(x, g, b):
    mu = jnp.mean(x, axis=-1, keepdims=True)
    xc = x - mu
    var = jnp.mean(xc * xc, axis=-1, keepdims=True)
    return xc * lax.rsqrt(var + LN_EPS) * g + b


def _ln0_kernel(x_ref, g_ref, b_ref, h_ref, hb_ref):
    h = _ln_rows(x_ref[...], g_ref[...], b_ref[...])
    h_ref[...] = h
    hb_ref[...] = h.astype(BF16)


def layer_norm0(x2, g, b, tm=256):
    t, d = x2.shape
    row = pl.BlockSpec((tm, d), lambda i: (i, 0))
    vec = pl.BlockSpec((1, d), lambda i: (0, 0))
    return pl.pallas_call(
        _ln0_kernel, grid=(t // tm,), in_specs=[row, vec, vec], out_specs=[row, row],
        out_shape=[jax.ShapeDtypeStruct((t, d), F32), jax.ShapeDtypeStruct((t, d), BF16)],
        compiler_params=_cparams(("parallel",)), name="ln0",
    )(x2, g.reshape(1, d), b.reshape(1, d))


def _rwkv_prep_kernel(has_vres, *refs):
    if has_vres:
        (z_ref, zp_ref, mu_ref, vec_ref, wa_hi_ref, wa_lo_ref, wg_hi_ref, wg_lo_ref, bd_ref, tri_ref, ones_ref,
         vf_ref, v0_ref, v1_ref, v2_ref,
         rt_ref, kap_ref, bt_ref, kt_ref, bh_ref, kh_ref, v_ref, bonus_ref, g_ref, pl8_ref) = refs
    else:
        (z_ref, zp_ref, mu_ref, vec_ref, wa_hi_ref, wa_lo_ref, wg_hi_ref, wg_lo_ref, bd_ref, tri_ref, ones_ref,
         rt_ref, kap_ref, bt_ref, kt_ref, bh_ref, kh_ref, v_ref, bonus_ref, g_ref, pl8_ref) = refs
    tm = z_ref.shape[0]
    first = pl.program_id(1) == 0

    z = z_ref[...].astype(F32)
    prev = zp_ref[SUBLANES - 1:SUBLANES, :].astype(F32)
    prev = jnp.where(first, 0.0, prev)
    row = lax.broadcasted_iota(jnp.int32, z.shape, 0)
    zs = jnp.where(row == 0, prev, pltpu.roll(z, 1, 0))
    z = z + (zs - z) * mu_ref[...]

    r = z[:, 0:RWKV_W]
    k = z[:, RWKV_W:2 * RWKV_W]
    v = z[:, 2 * RWKV_W:3 * RWKV_W]
    zl = z[:, 3 * RWKV_W:RW_PAD]
    col = lax.broadcasted_iota(jnp.int32, zl.shape, 1)
    act = jnp.where(col < DECAY_LORA, jnp.tanh(zl),
                    jnp.where(col < DECAY_LORA + AAA_LORA, zl, _sigmoid(zl)))

    def lora_dot(a, hi_ref, lo_ref):
        a_hi, a_lo = _split2(a)
        hi = hi_ref[...]
        return _dot(a_hi, hi) + _dot(a_lo, hi) + _dot(a_hi, lo_ref[...])

    lora_wa = lora_dot(act[:, :LANES], wa_hi_ref, wa_lo_ref)
    g = lora_dot(act[:, LANES:], wg_hi_ref, wg_lo_ref)
    w0 = vec_ref[0:1, :]
    a0 = vec_ref[1:2, :]
    k_k = vec_ref[2:3, :]
    k_a = vec_ref[3:4, :]
    r_k = vec_ref[4:5, :]
    x = -(w0 + lora_wa[:, 0:RWKV_W])
    softplus = jnp.maximum(x, 0.0) + jnp.log(1.0 + jnp.exp(-jnp.abs(x)))
    logd = -jnp.exp(-softplus - 0.5)
    a = _sigmoid(a0 + lora_wa[:, RWKV_W:])

    if has_vres:
        lo_rank = _dot(_dot(v.astype(BF16), v1_ref[...]).astype(BF16), v2_ref[...])
        v = v + (vf_ref[...].astype(F32) - v) * _sigmoid(v0_ref[...] + lo_rank)

    bd = bd_ref[...]

    def seg_sum(t):
        return jnp.concatenate(
            [_dot_x2(t[:, c:c + GROUP_W], bd) for c in range(0, RWKV_W, GROUP_W)], axis=1)

    kk = k * k_k
    kk = kk * lax.rsqrt(jnp.maximum(seg_sum(kk * kk), 1e-24))
    kmod = k * (1.0 + (a - 1.0) * k_a)
    b = kk * a
    bonus = seg_sum(r * kmod * r_k) * v

    l_hi, l_mid, l_lo = _split3(logd)
    tri = tri_ref[...]
    ones = ones_ref[...]
    c = _dot(tri, l_hi) + _dot(tri, l_mid) + _dot(tri, l_lo)
    cl = _dot(ones, l_hi) + _dot(ones, l_mid) + _dot(ones, l_lo)
    e_c = jnp.exp(c)
    e_prev = jnp.exp(c - logd)
    e_inv = jnp.exp(-c)
    e_rest = jnp.exp(cl - c)
    rt_ref[...] = (r * e_c).astype(BF16)
    kap_ref[...] = (kk * e_prev).astype(BF16)
    bt_ref[...] = (b * e_inv).astype(BF16)
    kt_ref[...] = (kmod * e_inv).astype(BF16)
    bh_ref[...] = (b * e_rest).astype(BF16)
    kh_ref[...] = (kmod * e_rest).astype(BF16)
    v_ref[...] = v.astype(BF16)
    bonus_ref[...] = bonus.astype(BF16)
    g_ref[...] = g.astype(BF16)
    e_cl = jnp.exp(cl)
    pl8_ref[...] = jnp.concatenate(
        [e_cl[j * CHUNK:j * CHUNK + SUBLANES] for j in range(tm // CHUNK)], axis=0)


def rwkv_prep(z_rw, batch, seq, mu, vecs, lora_w, vres, tm=256):
    t = z_rw.shape[0]
    nb = seq // tm
    has_vres = vres is not None
    ii = lax.broadcasted_iota(jnp.int32, (GROUP_W, GROUP_W), 0)
    jj = lax.broadcasted_iota(jnp.int32, (GROUP_W, GROUP_W), 1)
    bd = (ii // RWKV_HEAD == jj // RWKV_HEAD).astype(BF16)
    it = lax.broadcasted_iota(jnp.int32, (tm, tm), 0)
    jt = lax.broadcasted_iota(jnp.int32, (tm, tm), 1)
    same = it // CHUNK == jt // CHUNK
    tri = (same & (jt <= it)).astype(BF16)
    ones = same.astype(BF16)

    zrow = pl.BlockSpec((tm, RW_PAD), lambda b, i: (b * nb + i, 0))
    zprev = pl.BlockSpec(
        (SUBLANES, RW_PAD), lambda b, i: (jnp.maximum((b * nb + i) * (tm // SUBLANES) - 1, 0), 0))
    full = lambda shp: pl.BlockSpec(shp, lambda b, i: (0,) * len(shp))
    wrow = pl.BlockSpec((tm, RWKV_W), lambda b, i: (b * nb + i, 0))
    in_specs = [zrow, zprev, full((1, RW_PAD)), full((SUBLANES, RWKV_W)),
                full((LANES, 2 * RWKV_W)), full((LANES, 2 * RWKV_W)),
                full((LORA_PAD - LANES, RWKV_W)), full((LORA_PAD - LANES, RWKV_W)),
                full((GROUP_W, GROUP_W)), full((tm, tm)), full((tm, tm))]
    args = [z_rw, z_rw, mu, vecs, *lora_w, bd, tri, ones]
    if has_vres:
        v_first, v0, v1, v2 = vres
        in_specs += [wrow, full((1, RWKV_W)), full((RWKV_W, LANES)), full((LANES, RWKV_W))]
        args += [v_first, v0, v1, v2]
    p8 = tm // CHUNK * SUBLANES
    out_specs = [wrow] * 9 + [pl.BlockSpec((p8, RWKV_W), lambda b, i: (b * nb + i, 0))]
    out_shape = [jax.ShapeDtypeStruct((t, RWKV_W), BF16)] * 9 + [
        jax.ShapeDtypeStruct((t // CHUNK * SUBLANES, RWKV_W), F32)]
    return pl.pallas_call(
        functools.partial(_rwkv_prep_kernel, has_vres),
        grid=(batch, nb), in_specs=in_specs, out_specs=out_specs, out_shape=out_shape,
        compiler_params=_cparams(("parallel", "parallel")), name="rwkv_prep",
    )(*args)


def _rwkv_chunk_kernel(rt_ref, kap_ref, bt_ref, kt_ref, bh_ref, kh_ref, v_ref, pl8_ref,
                       bonus_ref, g_ref, lng_ref, lnb_ref, y_ref, h_ref):
    nbat, tb = rt_ref.shape[0], rt_ref.shape[1]
    w = GROUP_W

    @pl.when(pl.program_id(1) == 0)
    def _():
        h_ref[...] = jnp.zeros_like(h_ref)

    ri = lax.broadcasted_iota(jnp.int32, (w, w), 0)
    ci = lax.broadcasted_iota(jnp.int32, (w, w), 1)
    bd_mask = (ri >> 6) == (ci >> 6)
    eye = ri == ci
    tr = lax.broadcasted_iota(jnp.int32, (CHUNK, w), 0)
    sc = lax.broadcasted_iota(jnp.int32, (CHUNK, w), 1) & (CHUNK - 1)
    strict = sc < tr
    incl = sc <= tr
    seg_mean = jnp.where(bd_mask, 1.0 / RWKV_HEAD, 0.0).astype(BF16)

    def bd(y):
        return jnp.where(bd_mask, jnp.concatenate([y] * (w // CHUNK), axis=0), jnp.zeros((), y.dtype))

    def mm(x, y_bd):
        return _dot(x.astype(BF16), y_bd)

    nch = tb // CHUNK
    cs = range(nbat * nch)
    rows = [(c // nch, slice((c % nch) * CHUNK, (c % nch + 1) * CHUNK)) for c in cs]
    rt = [rt_ref[b, r, :] for b, r in rows]
    kap = [kap_ref[b, r, :] for b, r in rows]
    v = [v_ref[b, r, :] for b, r in rows]
    bh = [bh_ref[b, r, :] for b, r in rows]
    bt_in = [bt_ref[b, r, :] for b, r in rows]
    kt_in = [kt_ref[b, r, :] for b, r in rows]
    kh = [kh_ref[b, r, :] for b, r in rows]
    pl_row = [pl8_ref[c // nch, (c % nch) * SUBLANES:(c % nch) * SUBLANES + 1, :] for c in cs]
    lhs = [jnp.concatenate([kap[c], rt[c]], axis=0) for c in cs]
    a_b = [_dot_nt(lhs[c], bd(bt_in[c])) for c in cs]
    a_k = [_dot_nt(lhs[c], bd(kt_in[c])) for c in cs]
    a_rb = [jnp.where(incl, a_b[c][CHUNK:], 0.0).astype(BF16) for c in cs]
    a_kk = [jnp.concatenate([jnp.where(strict, a_k[c][:CHUNK], 0.0),
                             jnp.where(incl, a_k[c][CHUNK:], 0.0)], axis=0) for c in cs]

    n_pow = [-jnp.where(strict, a_b[c][:CHUNK], 0.0) for c in cs]
    t_inv = [jnp.where(sc == tr, 1.0, 0.0) + n_pow[c] for c in cs]
    n_pow = [mm(n_pow[c], bd(n_pow[c].astype(BF16))) for c in cs]
    for _ in range(4):
        both = [mm(jnp.concatenate([n_pow[c], t_inv[c]], axis=0), bd(n_pow[c].astype(BF16))) for c in cs]
        n_pow = [both[c][:CHUNK] for c in cs]
        t_inv = [t_inv[c] + both[c][CHUNK:] for c in cs]
    t_inv = [t_inv[c] + mm(t_inv[c], bd(n_pow[c].astype(BF16))) for c in cs]

    av = [mm(a_kk[c], bd(v[c])) for c in cs]
    t_bf = [t_inv[c].astype(BF16) for c in cs]
    tk = [_dot(t_bf[c], jnp.concatenate([bd(kap[c]), bd(av[c][:CHUNK].astype(BF16))], axis=1)).astype(BF16)
          for c in cs]
    ar = [_dot(a_rb[c], jnp.concatenate([bd(tk[c][:, :w]), bd(tk[c][:, w:])], axis=1)) for c in cs]
    r_hat = [(rt[c].astype(F32) - ar[c][:, :w]).astype(BF16) for c in cs]
    y_in = [av[c][CHUNK:] - ar[c][:, w:] for c in cs]
    lhs2 = [jnp.concatenate([r_hat[c], tk[c][:, :w]], axis=0) for c in cs]
    s_t = [jnp.concatenate([kh[c], bh[c]], axis=0).astype(F32).T.astype(BF16) for c in cs]
    pl_col = [jnp.sum(jnp.where(eye, pl_row[c], 0.0), axis=1, keepdims=True) for c in cs]

    h = [h_ref[b] for b in range(nbat)]
    y = [None] * len(cs)
    for ch in range(nch):
        for b in range(nbat):
            c = b * nch + ch
            h_hi, h_lo = _split2(h[b])
            uy = _dot(lhs2[c], h_hi) + _dot(lhs2[c], h_lo)
            y[c] = y_in[c] + uy[:CHUNK]
            r_mat = jnp.concatenate([v[c], (-(tk[c][:, w:].astype(F32) + uy[CHUNK:])).astype(BF16)], axis=0)
            h[b] = h[b] * pl_col[c] + jnp.where(bd_mask, _dot(s_t[c], r_mat), 0.0)
    for b in range(nbat):
        h_ref[b] = h[b]

    y_all = jnp.concatenate(y, axis=0)
    mu = _dot_x2(y_all, seg_mean)
    yc = y_all - mu
    var = _dot_x2(yc * yc, seg_mean)
    out = yc * lax.rsqrt(var + RWKV_GN_EPS) * lng_ref[...] + lnb_ref[...]
    for b in range(nbat):
        o = (out[b * tb:(b + 1) * tb] + bonus_ref[b].astype(F32)) * g_ref[b].astype(F32)
        y_ref[b] = o.astype(y_ref.dtype)


def rwkv_chunks(prep, batch, seq, lnx_g, lnx_b, tb=512):
    rt, kap, bt, kt, bh, kh, v, bonus, g, pl8 = prep
    nb = seq // tb
    ng = RWKV_W // GROUP_W
    p8_rows = tb // CHUNK * SUBLANES
    seq3 = lambda a: a.reshape(batch, a.shape[0] // batch, RWKV_W)
    blk = pl.BlockSpec((batch, tb, GROUP_W), lambda q, i: (0, i, q))
    p8 = pl.BlockSpec((batch, p8_rows, GROUP_W), lambda q, i: (0, i, q))
    vec = pl.BlockSpec((1, GROUP_W), lambda q, i: (0, q))
    y = pl.pallas_call(
        _rwkv_chunk_kernel,
        grid=(ng, nb),
        in_specs=[blk] * 7 + [p8, blk, blk, vec, vec],
        out_specs=blk,
        out_shape=jax.ShapeDtypeStruct((batch, seq, RWKV_W), BF16),
        scratch_shapes=[pltpu.VMEM((batch, GROUP_W, GROUP_W), F32)],
        compiler_params=_cparams(("parallel", "arbitrary")), name="rwkv_chunks",
    )(*[seq3(a) for a in (rt, kap, bt, kt, bh, kh, v, pl8, bonus, g)],
      lnx_g.reshape(1, RWKV_W), lnx_b.reshape(1, RWKV_W))
    return y.reshape(batch * seq, RWKV_W)


def _mla_prep_kernel(zq_ref, zkv_ref, cos_ref, sin_ref, qn_ref, kvn_ref, wq_ref,
                     wk_ref, wv_ref, e_ref, q_ref, k_ref, v_ref):
    zq = zq_ref[...].astype(F32)
    qn = (zq * lax.rsqrt(jnp.mean(zq * zq, axis=-1, keepdims=True) + RMS_EPS) * qn_ref[...]).astype(BF16)
    zkv = zkv_ref[:, 0:KV_LORA].astype(F32)
    cn = (zkv * lax.rsqrt(jnp.mean(zkv * zkv, axis=-1, keepdims=True) + RMS_EPS) * kvn_ref[...]).astype(BF16)
    kr = zkv_ref[:, KV_LORA:KVR_PAD]
    cos = cos_ref[...]
    sin = sin_ref[...]
    q_all = _dot(qn, wq_ref[...])
    k_all = _dot(cn, wk_ref[...]) + _dot(kr, e_ref[...])

    def rotary(t2):
        return t2 * cos + pltpu.roll(t2, QK_ROPE, 1) * sin

    for h in range(MLA_HEADS):
        nope = slice(h * QK_PAD, h * QK_PAD + QK_NOPE)
        rope = slice(h * QK_PAD + QK_NOPE, (h + 1) * QK_PAD)
        q_ref[:, nope] = (q_all[:, nope] * ATTN_SCALE).astype(BF16)
        q_ref[:, rope] = (rotary(q_all[:, rope]) * ATTN_SCALE).astype(BF16)
        k_ref[:, nope] = k_all[:, nope].astype(BF16)
        k_ref[:, rope] = rotary(k_all[:, rope]).astype(BF16)
    v_ref[...] = _dot_nt(wv_ref[...], cn).astype(BF16)


def mla_prep(z_q, z_kvr, cos_t, sin_t, qn, kvn, wq, wk, wv_t, e_mat, tm=ATTN_TILE):
    t = z_q.shape[0]
    hw = MLA_HEADS * QK_PAD
    vw = MLA_HEADS * V_HEAD
    row = lambda w: pl.BlockSpec((tm, w), lambda i: (i, 0))
    full = lambda a: pl.BlockSpec(a.shape, lambda i: (0, 0))
    return pl.pallas_call(
        _mla_prep_kernel, grid=(t // tm,),
        in_specs=[row(Q_LORA), row(KVR_PAD), row(LANES), row(LANES), full(qn), full(kvn),
                  full(wq), full(wk), full(wv_t), full(e_mat)],
        out_specs=[row(hw), row(hw), pl.BlockSpec((None, vw, tm), lambda i: (i, 0, 0))],
        out_shape=[jax.ShapeDtypeStruct((t, hw), BF16), jax.ShapeDtypeStruct((t, hw), BF16),
                   jax.ShapeDtypeStruct((t // tm, vw, tm), BF16)],
        compiler_params=_cparams(("parallel",)), name="mla_prep",
    )(z_q, z_kvr, cos_t, sin_t, qn, kvn, wq, wk, wv_t, e_mat)


def _attn_kernel(q_ref, k_ref, vt_ref, o_ref, st0_ref, st1_ref, mx0_ref, mx1_ref, m_ref, acc_ref):
    tq = q_ref.shape[0]
    tk = tq
    i = pl.program_id(2)
    q = q_ref[...]
    ones = jnp.ones((BF16_ROWS, tk), BF16)

    def qk(j, st_ref, mx_ref):
        start = pl.multiple_of(j * tk, tk)
        st = _dot_nt(k_ref[pl.ds(start, tk), :], q)
        st_ref[...] = st
        mx_ref[...] = jnp.max(st, axis=0, keepdims=True)

    def consume(j, st_ref, mx_ref, masked):
        st = st_ref[...]
        if masked:
            ki = lax.broadcasted_iota(jnp.int32, st.shape, 0)
            qi = lax.broadcasted_iota(jnp.int32, st.shape, 1)
            st = jnp.where(ki <= qi, st, NEG_BIG)
            mx = jnp.max(st, axis=0, keepdims=True)
        else:
            mx = mx_ref[...]
        m = m_ref[...]
        m_new = jnp.maximum(m, mx)
        alpha = jnp.exp(m - m_new)
        p = jnp.exp(st - m_new).astype(BF16)
        vt = jnp.concatenate([vt_ref[j], ones], axis=0)
        acc_ref[...] = alpha * acc_ref[...] + _dot(vt, p)
        m_ref[...] = m_new

    m_ref[...] = jnp.full(m_ref.shape, NEG_BIG, F32)
    acc_ref[...] = jnp.zeros(acc_ref.shape, F32)
    qk(0, st0_ref, mx0_ref)

    def pair(jj, carry):
        j = 2 * jj
        qk(j + 1, st1_ref, mx1_ref)
        consume(j, st0_ref, mx0_ref, False)
        qk(j + 2, st0_ref, mx0_ref)
        consume(j + 1, st1_ref, mx1_ref, False)
        return carry

    lax.fori_loop(0, i // 2, pair, 0)

    @pl.when(i % 2 == 0)
    def _():
        consume(i, st0_ref, mx0_ref, True)

    @pl.when(i % 2 == 1)
    def _():
        qk(i, st1_ref, mx1_ref)
        consume(i - 1, st0_ref, mx0_ref, False)
        consume(i, st1_ref, mx1_ref, True)

    acc = acc_ref[...]
    out = acc[:V_HEAD] / acc[V_HEAD:V_HEAD + 1]
    o_ref[...] = out.T.astype(o_ref.dtype)


def attention(q_all, k_all, vt_all, batch, seq, tq=ATTN_TILE):
    t = q_all.shape[0]
    nq = seq // tq
    return pl.pallas_call(
        _attn_kernel,
        grid=(batch, MLA_HEADS, nq),
        in_specs=[pl.BlockSpec((tq, QK_PAD), lambda b, h, i: (b * nq + i, h)),
                  pl.BlockSpec((seq, QK_PAD), lambda b, h, i: (b, h)),
                  pl.BlockSpec((nq, V_HEAD, tq), lambda b, h, i: (b, h, 0))],
        out_specs=pl.BlockSpec((tq, V_HEAD), lambda b, h, i: (b * nq + i, h)),
        out_shape=jax.ShapeDtypeStruct((t, MLA_HEADS * V_HEAD), BF16),
        scratch_shapes=[pltpu.VMEM((tq, tq), F32), pltpu.VMEM((tq, tq), F32),
                        pltpu.VMEM((1, tq), F32), pltpu.VMEM((1, tq), F32),
                        pltpu.VMEM((1, tq), F32), pltpu.VMEM((V_HEAD + BF16_ROWS, tq), F32)],
        compiler_params=_cparams(("parallel", "parallel", "arbitrary")), name="attention",
    )(q_all, k_all, vt_all)


CONV_HALO = 32


def _conv_kernel(z_ref, zh_ref, dw_ref, db_ref, g_ref, b_ref, o_ref, u_ref, s_ref):
    tm = z_ref.shape[0]
    first = pl.program_id(1) == 0

    def glu(zz):
        zz = zz.astype(F32)
        return zz[:, :CONV_W] * _sigmoid(zz[:, CONV_W:])

    u_ref[0:CONV_HALO, :] = jnp.where(first, 0.0, glu(zh_ref[...]))
    u_ref[CONV_HALO:, :] = glu(z_ref[...])
    off = CONV_HALO - (CONV_K - 1)
    acc = jnp.zeros((tm, CONV_W), F32) + db_ref[...]
    for res in range(SUBLANES):
        taps = [j for j in range(CONV_K) if (off + j) % SUBLANES == res]
        span = max(off + j for j in taps) - res + tm
        if res == 0:
            src_ref = u_ref
        else:
            s_ref[0:span, :] = u_ref[res:res + span, :]
            src_ref = s_ref
        for j in taps:
            lo = off + j - res
            acc = acc + dw_ref[j:j + 1, :] * src_ref[lo:lo + tm, :]
    y = _ln_rows(acc, g_ref[...], b_ref[...])
    o_ref[...] = (y * _sigmoid(y)).astype(o_ref.dtype)


def conformer_conv(z_conv, batch, seq, dw, db, ln_g, ln_b, tm=256):
    t = z_conv.shape[0]
    nb = seq // tm
    ratio = tm // CONV_HALO
    vec = pl.BlockSpec((1, CONV_W), lambda b, i: (0, 0))
    return pl.pallas_call(
        _conv_kernel, grid=(batch, nb),
        in_specs=[pl.BlockSpec((tm, 2 * CONV_W), lambda b, i: (b * nb + i, 0)),
                  pl.BlockSpec((CONV_HALO, 2 * CONV_W),
                               lambda b, i: (jnp.maximum((b * nb + i) * ratio - 1, 0), 0)),
                  pl.BlockSpec((CONV_HALO, CONV_W), lambda b, i: (0, 0)), vec, vec, vec],
        out_specs=pl.BlockSpec((tm, CONV_W), lambda b, i: (b * nb + i, 0)),
        out_shape=jax.ShapeDtypeStruct((t, CONV_W), BF16),
        scratch_shapes=[pltpu.VMEM((tm + CONV_HALO, CONV_W), F32), pltpu.VMEM((tm + CONV_HALO, CONV_W), F32)],
        compiler_params=_cparams(("parallel", "parallel")), name="conformer_conv",
    )(z_conv, z_conv, dw, db.reshape(1, CONV_W), ln_g.reshape(1, CONV_W), ln_b.reshape(1, CONV_W))


def _merge_kernel(ya_ref, yb_ref, yc_ref, ga_ref, gb_ref, gc_ref, wb_ref, o_ref):
    acc = None
    for i, (y_ref, zg_ref) in enumerate(((ya_ref, ga_ref), (yb_ref, gb_ref), (yc_ref, gc_ref))):
        term = _sigmoid(zg_ref[...].astype(F32)) * _dot(y_ref[...], wb_ref[i])
        acc = term if acc is None else acc + term
    o_ref[...] = acc.astype(o_ref.dtype)


def branch_merge(y_a, y_b, y_c, zg, wb, tm=512, tn=1024):
    t = y_a.shape[0]
    yspec = pl.BlockSpec((tm, RWKV_W), lambda j, i: (i, 0))
    gspec = pl.BlockSpec((tm, tn), lambda j, i: (i, j))
    return pl.pallas_call(
        _merge_kernel, grid=(D_MODEL // tn, t // tm),
        in_specs=[yspec, yspec, yspec, gspec, gspec, gspec,
                  pl.BlockSpec((N_BRANCH, RWKV_W, tn), lambda j, i: (0, 0, j))],
        out_specs=pl.BlockSpec((tm, tn), lambda j, i: (i, j)),
        out_shape=jax.ShapeDtypeStruct((t, D_MODEL), BF16),
        compiler_params=_cparams(("parallel", "parallel")), name="branch_merge",
    )(y_a, y_b, y_c, *zg, wb)


def _out_ln_router_kernel(m_ref, w_ref, h_ref, g_ref, b_ref, rw_hi_ref, rw_lo_ref, rb_ref, tri_ref,
                          ho_ref, hb_ref, ei_ref, gw_ref, cnt_out_ref, cnt_ref):
    x = DN_ALPHA * h_ref[...] + _dot(m_ref[...], w_ref[...])
    h = _ln_rows(x, g_ref[...], b_ref[...])
    ho_ref[...] = h
    hb_ref[...] = h.astype(BF16)
    h_hi, h_lo = _split2(h)
    rw_hi = rw_hi_ref[...]
    logits = _dot_nt(rw_hi, h_hi) + _dot_nt(rw_hi, h_lo) + _dot_nt(rw_lo_ref[...], h_hi)
    scores = _sigmoid(logits)
    sel = scores + rb_ref[...]
    tm = sel.shape[1]
    idx = lax.broadcasted_iota(jnp.int32, (EXP_PER_GROUP, tm), 0)

    def first_max(x):
        mx = jnp.max(x, axis=0, keepdims=True)
        return mx, jnp.min(jnp.where(x == mx, idx, EXP_PER_GROUP), axis=0, keepdims=True)

    best = None
    for grp in range(N_GROUPS):
        rows = slice(grp * EXP_PER_GROUP, (grp + 1) * EXP_PER_GROUP)
        x = sel[rows]
        sc_g = scores[rows]
        m1, i1 = first_max(x)
        m2, i2 = first_max(jnp.where(idx == i1, -jnp.inf, x))
        s1 = jnp.sum(jnp.where(idx == i1, sc_g, 0.0), axis=0, keepdims=True)
        s2 = jnp.sum(jnp.where(idx == i2, sc_g, 0.0), axis=0, keepdims=True)
        cand = (m1 + m2, i1 + grp * EXP_PER_GROUP, i2 + grp * EXP_PER_GROUP, s1, s2)
        if best is None:
            best = cand
        else:
            take = cand[0] > best[0]
            best = tuple(jnp.where(take, cn, bs) for cn, bs in zip(cand, best))
    _, e1, e2, s1, s2 = best
    @pl.when(pl.program_id(0) == 0)
    def _():
        cnt_ref[...] = jnp.zeros_like(cnt_ref)

    erow = lax.broadcasted_iota(jnp.int32, (N_EXPERTS, tm), 0)
    hit1 = erow == e1
    hit2 = erow == e2
    onehot = jnp.where(hit1, 1.0, jnp.where(hit2, 1.0, 0.0))
    before = cnt_ref[:, 0:1] + _dot(onehot.astype(BF16), tri_ref[...]) - onehot
    rank1 = jnp.sum(jnp.where(hit1, before, 0.0), axis=0, keepdims=True).astype(jnp.int32)
    rank2 = jnp.sum(jnp.where(hit2, before, 0.0), axis=0, keepdims=True).astype(jnp.int32)
    cnt_ref[...] = cnt_ref[...] + jnp.sum(onehot, axis=1, keepdims=True)
    cnt_out_ref[...] = cnt_ref[...]
    row = lax.broadcasted_iota(jnp.int32, (SUBLANES, tm), 0)
    ei_ref[...] = jnp.where(row == 0, e1, jnp.where(row == 1, e2,
                                                    jnp.where(row == 2, rank1, jnp.where(row == 3, rank2, 0))))
    inv = 1.0 / (s1 + s2)
    gw_ref[...] = jnp.where(row == 0, s1 * inv, jnp.where(row == 1, s2 * inv, 0.0))


def out_ln_router(merged, w_out, h, g, b, rw_hi, rw_lo, rb, tm=512):
    t, d = h.shape
    si = lax.broadcasted_iota(jnp.int32, (tm, tm), 0)
    ti = lax.broadcasted_iota(jnp.int32, (tm, tm), 1)
    tri = (si <= ti).astype(BF16)
    row = pl.BlockSpec((tm, d), lambda i: (i, 0))
    vec = pl.BlockSpec((1, d), lambda i: (0, 0))
    rws = pl.BlockSpec((N_EXPERTS, d), lambda i: (0, 0))
    tok = pl.BlockSpec((SUBLANES, tm), lambda i: (0, i))
    return pl.pallas_call(
        _out_ln_router_kernel, grid=(t // tm,),
        in_specs=[row, pl.BlockSpec((d, d), lambda i: (0, 0)), row, vec, vec, rws, rws,
                  pl.BlockSpec((N_EXPERTS, tm), lambda i: (0, 0)), pl.BlockSpec((tm, tm), lambda i: (0, 0))],
        out_specs=[row, row, tok, tok, pl.BlockSpec((N_EXPERTS, LANES), lambda i: (0, 0))],
        out_shape=[jax.ShapeDtypeStruct((t, d), F32), jax.ShapeDtypeStruct((t, d), BF16),
                   jax.ShapeDtypeStruct((SUBLANES, t), jnp.int32), jax.ShapeDtypeStruct((SUBLANES, t), F32),
                   jax.ShapeDtypeStruct((N_EXPERTS, LANES), F32)],
        scratch_shapes=[pltpu.VMEM((N_EXPERTS, LANES), F32)],
        compiler_params=_cparams(("arbitrary",)), name="out_ln_router",
    )(merged, w_out, h, g.reshape(1, d), b.reshape(1, d), rw_hi, rw_lo,
      jnp.broadcast_to(rb.reshape(N_EXPERTS, 1), (N_EXPERTS, tm)), tri)


def _moe_kernel(be_ref, nv_ref, x_ref, w13_ref, w2_ref, o_ref, w13b_ref, w2b_ref):
    i = pl.program_id(0)

    @pl.when((i == 0) | (be_ref[i] != be_ref[jnp.maximum(i - 1, 0)]))
    def _():
        w13b_ref[...] = w13_ref[...].astype(BF16)
        w2b_ref[...] = w2_ref[...].astype(BF16)

    @pl.when(i < nv_ref[0])
    def _():
        gu = _dot(x_ref[...], w13b_ref[...])
        gate = gu[:, :D_EXPERT]
        act = gate * _sigmoid(gate) * gu[:, D_EXPERT:]
        o_ref[...] = _dot(act.astype(BF16), w2b_ref[...]).astype(o_ref.dtype)

    @pl.when(i >= nv_ref[0])
    def _():
        o_ref[...] = jnp.zeros_like(o_ref)


def moe_experts(x_sorted, block_e, n_valid, w13, w2, layer):
    p, d = x_sorted.shape
    nblk = p // MOE_TB
    grid_spec = pltpu.PrefetchScalarGridSpec(
        num_scalar_prefetch=2, grid=(nblk,),
        in_specs=[pl.BlockSpec((MOE_TB, d), lambda i, be, nv: (i, 0)),
                  pl.BlockSpec((None, None, d, 2 * D_EXPERT), lambda i, be, nv: (layer, be[i], 0, 0)),
                  pl.BlockSpec((None, None, D_EXPERT, d), lambda i, be, nv: (layer, be[i], 0, 0))],
        out_specs=pl.BlockSpec((MOE_TB, d), lambda i, be, nv: (i, 0)),
        scratch_shapes=[pltpu.VMEM((d, 2 * D_EXPERT), BF16), pltpu.VMEM((D_EXPERT, d), BF16)])
    return pl.pallas_call(
        _moe_kernel, grid_spec=grid_spec,
        out_shape=jax.ShapeDtypeStruct((p, d), BF16),
        compiler_params=_cparams(("arbitrary",)), name="moe_experts",
    )(block_e, n_valid, x_sorted, w13, w2)


def _combine_ln_kernel(final, h_ref, y0_ref, y1_ref, gw_ref, g_ref, b_ref, *outs):
    gw = gw_ref[...]
    f = gw[:, 0:1] * y0_ref[...].astype(F32) + gw[:, 1:2] * y1_ref[...].astype(F32)
    h = _ln_rows(DN_ALPHA * h_ref[...] + f, g_ref[...], b_ref[...])
    outs[0][...] = h
    if not final:
        outs[1][...] = h.astype(BF16)


def combine_ln(h, y0, y1, gate_w, g, b, final, tm=256):
    t, d = h.shape
    row = pl.BlockSpec((tm, d), lambda i: (i, 0))
    vec = pl.BlockSpec((1, d), lambda i: (0, 0))
    out_specs = [row] if final else [row, row]
    out_shape = [jax.ShapeDtypeStruct((t, d), F32)] + ([] if final else [jax.ShapeDtypeStruct((t, d), BF16)])
    return pl.pallas_call(
        functools.partial(_combine_ln_kernel, final), grid=(t // tm,),
        in_specs=[row, row, row, pl.BlockSpec((tm, LANES), lambda i: (i, 0)), vec, vec],
        out_specs=out_specs, out_shape=out_shape,
        compiler_params=_cparams(("parallel",)), name="combine_ln",
    )(h, y0, y1, gate_w, g.reshape(1, d), b.reshape(1, d))


def _route(e_rows, gate_rows, counts):
    e_idx = e_rows[:TOP_K].T
    rank = e_rows[TOP_K:2 * TOP_K].T.reshape(-1)
    gate = gate_rows[:TOP_K].T
    t = e_idx.shape[0]
    a = t * TOP_K
    flat_e = e_idx.reshape(a)
    counts = counts[:, 0].astype(jnp.int32)
    padded = (counts + MOE_TB - 1) // MOE_TB * MOE_TB
    ends = jnp.cumsum(padded)
    pstarts = ends - padded
    e_hot = flat_e[:, None] == jnp.arange(N_EXPERTS, dtype=jnp.int32)[None, :]
    dest = jnp.sum(jnp.where(e_hot, pstarts[None, :], 0), axis=1) + rank
    p = a + N_EXPERTS * MOE_TB
    nblk = p // MOE_TB
    flat_tok = jnp.repeat(jnp.arange(t, dtype=jnp.int32), TOP_K)
    tok_pad = jnp.zeros((p,), jnp.int32).at[dest].set(flat_tok)
    block_start = jnp.arange(nblk, dtype=jnp.int32) * MOE_TB
    block_e = jnp.minimum(jnp.sum((ends[None, :] <= block_start[:, None]).astype(jnp.int32), axis=1),
                          N_EXPERTS - 1)
    n_valid = (ends[-1] // MOE_TB).astype(jnp.int32).reshape(1)
    dest2 = dest.reshape(t, TOP_K)
    gate_w = jnp.pad(gate, ((0, 0), (0, LANES - TOP_K)))
    return tok_pad, block_e, n_valid, dest2[:, 0], dest2[:, 1], gate_w


def _pad_cols(w, n):
    return jnp.pad(w, ((0, 0), (0, n - w.shape[1])))


def _pad_rows(w, n):
    return jnp.pad(w, ((0, n - w.shape[0]), (0, 0)))


def _rope_tables(positions):
    inv = ROPE_THETA ** (-jnp.arange(0, QK_ROPE, 2, dtype=F32) / QK_ROPE)
    ang = positions.astype(F32).reshape(-1)[:, None] * inv
    cos, sin = jnp.cos(ang), jnp.sin(ang)
    zeros = jnp.zeros((ang.shape[0], LANES - QK_ROPE), F32)
    return jnp.concatenate([cos, cos, zeros], axis=1), jnp.concatenate([sin, sin, zeros], axis=1)


def _rot_cols(w):
    half = QK_ROPE // 2
    return jnp.concatenate([-w[..., half:], w[..., :half]], axis=-1)


def _mla_weights(q_b, kv_b):
    qb = q_b.reshape(Q_LORA, MLA_HEADS, QK_NOPE + QK_ROPE)
    wq = jnp.concatenate([qb, _rot_cols(qb[..., QK_NOPE:])], axis=-1).reshape(Q_LORA, MLA_HEADS * QK_PAD)
    kvb = kv_b.reshape(KV_LORA, MLA_HEADS, QK_NOPE + V_HEAD)
    wk = jnp.concatenate([kvb[..., :QK_NOPE], jnp.zeros((KV_LORA, MLA_HEADS, QK_PAD - QK_NOPE), F32)],
                         axis=-1).reshape(KV_LORA, MLA_HEADS * QK_PAD)
    wv = kvb[..., QK_NOPE:].reshape(KV_LORA, MLA_HEADS * V_HEAD).T
    eye = jnp.eye(QK_ROPE, dtype=F32)
    e_mat = _pad_rows(jnp.tile(jnp.concatenate([jnp.zeros((QK_ROPE, QK_NOPE), F32), eye, _rot_cols(eye)],
                                               axis=1), (1, MLA_HEADS)), KVR_PAD - KV_LORA)
    return tuple(m.astype(BF16) for m in (wq, wk, wv, e_mat))


def kernel(x, positions, ln0_g, ln0_b, w_in, rw_mu, rw_w0, rw_w2, rw_a0, rw_a2, rw_g2, rw_kk, rw_ka, rw_rk, rw_lnx_g, rw_lnx_b, rw_v0, rw_v1, rw_v2, mla_q_norm, mla_q_b, mla_kv_norm, mla_kv_b, conv_dw, conv_db, conv_ln_g, conv_ln_b, w_branch, w_out, ln1_g, ln1_b, router_w, router_b, moe_w13, moe_w2, ln2_g, ln2_b):
    batch, seq, d = x.shape
    t = batch * seq
    h, hb = layer_norm0(x.reshape(t, d), ln0_g, ln0_b)
    cos_t, sin_t = _rope_tables(positions)
    rw_t = router_w.T
    rw_hi = rw_t.astype(BF16)
    rw_lo = (rw_t - rw_hi.astype(F32)).astype(BF16)
    o_q = RW_SHIFT
    o_kv = o_q + Q_LORA
    o_conv = o_kv + KV_LORA + QK_ROPE
    o_gate = o_conv + 2 * CONV_W
    v_first = None
    for l in range(DEPTH):
        wi = w_in[l]
        w_rw = _pad_cols(wi[:, :RW_SHIFT], RW_PAD).astype(BF16)
        w_q = wi[:, o_q:o_kv].astype(BF16)
        w_kvr = _pad_cols(wi[:, o_kv:o_conv], KVR_PAD).astype(BF16)
        w_conv = wi[:, o_conv:o_gate].astype(BF16)
        w_gate = wi[:, o_gate:].astype(BF16)

        z_rw = matmul(hb, w_rw, 512, RW_PAD // 3)
        mu = _pad_cols(rw_mu[l].reshape(1, RW_SHIFT), RW_PAD)
        vecs = _pad_rows(jnp.stack([rw_w0[l], rw_a0[l], rw_kk[l], rw_ka[l], rw_rk[l]]), SUBLANES)
        zw2 = jnp.zeros((DECAY_LORA, RWKV_W), F32)
        w_wa = jnp.concatenate([jnp.concatenate([rw_w2[l], zw2], axis=1),
                                jnp.concatenate([zw2, rw_a2[l]], axis=1)], axis=0)
        w_g = _pad_rows(rw_g2[l], LORA_PAD - LANES)
        lora_w = []
        for wmat in (w_wa, w_g):
            w_hi = wmat.astype(BF16)
            lora_w += [w_hi, (wmat - w_hi.astype(F32)).astype(BF16)]
        if l == 0:
            vres = None
        else:
            vres = (v_first, rw_v0[l - 1].reshape(1, RWKV_W),
                    _pad_cols(rw_v1[l - 1], LANES).astype(BF16), _pad_rows(rw_v2[l - 1], LANES).astype(BF16))
        prep = rwkv_prep(z_rw, batch, seq, mu, vecs, lora_w, vres)
        if l == 0:
            v_first = prep[6]
        y_a = rwkv_chunks(prep, batch, seq, rw_lnx_g[l], rw_lnx_b[l])

        z_q = matmul(hb, w_q, 512, Q_LORA)
        z_kvr = matmul(hb, w_kvr, 512, KVR_PAD)
        mw = _mla_weights(mla_q_b[l], mla_kv_b[l])
        q_all, k_all, v_all = mla_prep(z_q, z_kvr, cos_t, sin_t, mla_q_norm[l].reshape(1, Q_LORA),
                                       mla_kv_norm[l].reshape(1, KV_LORA), *mw)
        y_b = attention(q_all, k_all, v_all, batch, seq)

        z_conv = matmul(hb, w_conv, 512, CONV_W)
        y_c = conformer_conv(z_conv, batch, seq, _pad_rows(conv_dw[l], CONV_HALO), conv_db[l],
                             conv_ln_g[l], conv_ln_b[l])

        z_gate = [matmul(hb, w_gate[:, i * d:(i + 1) * d], 512, 1024) for i in range(N_BRANCH)]
        merged = branch_merge(y_a, y_b, y_c, z_gate, w_branch[l].astype(BF16))
        h, hb, e_rows, gate_rows, counts = out_ln_router(merged, w_out[l].astype(BF16), h, ln1_g[l], ln1_b[l],
                                                         rw_hi, rw_lo, router_b)

        tok_pad, block_e, n_valid, p0, p1, gate_w = _route(e_rows, gate_rows, counts)
        ys = moe_experts(hb[tok_pad], block_e, n_valid, moe_w13, moe_w2, l)
        final = l == DEPTH - 1
        res = combine_ln(h, ys[p0], ys[p1], gate_w, ln2_g[l], ln2_b[l], final)
        if final:
            h = res[0]
        else:
            h, hb = res
    return h.reshape(batch, seq, d)
```

```python
import functools

import jax
import jax.numpy as jnp
from jax import lax
from jax.experimental import pallas as pl
from jax.experimental.pallas import tpu as pltpu

F32 = jnp.float32
BF16 = jnp.bfloat16

D_MODEL = 2048
DEPTH = 4
RWKV_HEADS = 16
RWKV_HEAD = 64
RWKV_W = 1024
DECAY_LORA = 64
AAA_LORA = 64
MV_LORA = 32
GATE_LORA = 160
RWKV_GN_EPS = 64e-5
MLA_HEADS = 8
Q_LORA = 512
KV_LORA = 256
QK_NOPE = 128
QK_ROPE = 64
V_HEAD = 128
ROPE_THETA = 10000.0
ATTN_SCALE = (QK_NOPE + QK_ROPE) ** -0.5
CONV_W = 1024
CONV_K = 31
N_BRANCH = 3
N_EXPERTS = 32
N_GROUPS = 4
EXP_PER_GROUP = 8
TOP_K = 2
D_EXPERT = 512
DN_ALPHA = (2 * DEPTH) ** 0.25
LN_EPS = 1e-5
RMS_EPS = 1e-6
RW_SHIFT = 3 * RWKV_W + DECAY_LORA + AAA_LORA + GATE_LORA
LORA_W = DECAY_LORA + AAA_LORA + GATE_LORA

LANES = 128
SUBLANES = 8
VMEM_LIMIT = 48 * 1024 * 1024

LORA_PAD = 384
RW_PAD = 3 * RWKV_W + LORA_PAD
KVR_PAD = 384
QK_PAD = 256

BF16_ROWS = 16
ATTN_TILE = 1024
VT_TILE = 512
CHUNK = 64
GROUP_W = 256
MOE_TB = 512
NEG_BIG = -1e30


def _cparams(sem):
    return pltpu.CompilerParams(dimension_semantics=sem, vmem_limit_bytes=VMEM_LIMIT)


def _dot(a, b):
    return jnp.dot(a, b, preferred_element_type=F32)


def _dot_nt(a, b):
    return lax.dot_general(a, b, (((1,), (1,)), ((), ())), preferred_element_type=F32)


def _dot_tn(a, b):
    return lax.dot_general(a, b, (((0,), (0,)), ((), ())), preferred_element_type=F32)


def _split2(x):
    hi = x.astype(BF16)
    lo = (x - hi.astype(F32)).astype(BF16)
    return hi, lo


def _split3(x):
    hi = x.astype(BF16)
    r1 = x - hi.astype(F32)
    mid = r1.astype(BF16)
    lo = (r1 - mid.astype(F32)).astype(BF16)
    return hi, mid, lo


def _dot_x2(x, w_bf16):
    hi, lo = _split2(x)
    return _dot(hi, w_bf16) + _dot(lo, w_bf16)


def _sigmoid(x):
    return 1.0 / (1.0 + jnp.exp(-x))


def _mm_kernel(a_ref, b_ref, o_ref):
    o_ref[...] = _dot(a_ref[...], b_ref[...]).astype(o_ref.dtype)


def matmul(a, b, tm, tn, out_dtype=BF16):
    m, k = a.shape
    n = b.shape[1]
    return pl.pallas_call(
        _mm_kernel,
        grid=(n // tn, m // tm),
        in_specs=[pl.BlockSpec((tm, k), lambda j, i: (i, 0)),
                  pl.BlockSpec((k, tn), lambda j, i: (0, j))],
        out_specs=pl.BlockSpec((tm, tn), lambda j, i: (i, j)),
        out_shape=jax.ShapeDtypeStruct((m, n), out_dtype),
        compiler_params=_cparams(("parallel", "parallel")),
        name="matmul",
    )(a, b)


def _ln_rows(x, g, b):
    mu = jnp.mean(x, axis=-1, keepdims=True)
    xc = x - mu
    var = jnp.mean(xc * xc, axis=-1, keepdims=True)
    return xc * lax.rsqrt(var + LN_EPS) * g + b


def _ln0_kernel(x_ref, g_ref, b_ref, h_ref, hb_ref):
    h = _ln_rows(x_ref[...], g_ref[...], b_ref[...])
    h_ref[...] = h
    hb_ref[...] = h.astype(BF16)


def layer_norm0(x2, g, b, tm=256):
    t, d = x2.shape
    row = pl.BlockSpec((tm, d), lambda i: (i, 0))
    vec = pl.BlockSpec((1, d), lambda i: (0, 0))
    return pl.pallas_call(
        _ln0_kernel, grid=(t // tm,), in_specs=[row, vec, vec], out_specs=[row, row],
        out_shape=[jax.ShapeDtypeStruct((t, d), F32), jax.ShapeDtypeStruct((t, d), BF16)],
        compiler_params=_cparams(("parallel",)), name="ln0",
    )(x2, g.reshape(1, d), b.reshape(1, d))


def _rwkv_prep_kernel(has_vres, *refs):
    if has_vres:
        (z_ref, zp_ref, mu_ref, vec_ref, wa_hi_ref, wa_lo_ref, wg_hi_ref, wg_lo_ref, bd_ref, tri_ref, ones_ref,
         vf_ref, v0_ref, v1_ref, v2_ref,
         rt_ref, kap_ref, bt_ref, kt_ref, bh_ref, kh_ref, v_ref, bonus_ref, g_ref, pl8_ref) = refs
    else:
        (z_ref, zp_ref, mu_ref, vec_ref, wa_hi_ref, wa_lo_ref, wg_hi_ref, wg_lo_ref, bd_ref, tri_ref, ones_ref,
         rt_ref, kap_ref, bt_ref, kt_ref, bh_ref, kh_ref, v_ref, bonus_ref, g_ref, pl8_ref) = refs
    tm = z_ref.shape[0]
    first = pl.program_id(1) == 0

    z = z_ref[...].astype(F32)
    prev = zp_ref[SUBLANES - 1:SUBLANES, :].astype(F32)
    prev = jnp.where(first, 0.0, prev)
    row = lax.broadcasted_iota(jnp.int32, z.shape, 0)
    zs = jnp.where(row == 0, prev, pltpu.roll(z, 1, 0))
    z = z + (zs - z) * mu_ref[...]

    r = z[:, 0:RWKV_W]
    k = z[:, RWKV_W:2 * RWKV_W]
    v = z[:, 2 * RWKV_W:3 * RWKV_W]
    zl = z[:, 3 * RWKV_W:RW_PAD]
    col = lax.broadcasted_iota(jnp.int32, zl.shape, 1)
    act = jnp.where(col < DECAY_LORA, jnp.tanh(zl),
                    jnp.where(col < DECAY_LORA + AAA_LORA, zl, _sigmoid(zl)))

    def lora_dot(a, hi_ref, lo_ref):
        a_hi, a_lo = _split2(a)
        hi = hi_ref[...]
        return _dot(a_hi, hi) + _dot(a_lo, hi) + _dot(a_hi, lo_ref[...])

    lora_wa = lora_dot(act[:, :LANES], wa_hi_ref, wa_lo_ref)
    g = lora_dot(act[:, LANES:], wg_hi_ref, wg_lo_ref)
    w0 = vec_ref[0:1, :]
    a0 = vec_ref[1:2, :]
    k_k = vec_ref[2:3, :]
    k_a = vec_ref[3:4, :]
    r_k = vec_ref[4:5, :]
    x = -(w0 + lora_wa[:, 0:RWKV_W])
    softplus = jnp.maximum(x, 0.0) + jnp.log(1.0 + jnp.exp(-jnp.abs(x)))
    logd = -jnp.exp(-softplus - 0.5)
    a = _sigmoid(a0 + lora_wa[:, RWKV_W:])

    if has_vres:
        lo_rank = _dot(_dot(v.astype(BF16), v1_ref[...]).astype(BF16), v2_ref[...])
        v = v + (vf_ref[...].astype(F32) - v) * _sigmoid(v0_ref[...] + lo_rank)

    bd = bd_ref[...]

    def seg_sum(t):
        return jnp.concatenate(
            [_dot_x2(t[:, c:c + GROUP_W], bd) for c in range(0, RWKV_W, GROUP_W)], axis=1)

    kk = k * k_k
    kk = kk * lax.rsqrt(jnp.maximum(seg_sum(kk * kk), 1e-24))
    kmod = k * (1.0 + (a - 1.0) * k_a)
    b = kk * a
    bonus = seg_sum(r * kmod * r_k) * v

    l_hi, l_mid, l_lo = _split3(logd)
    tri = tri_ref[...]
    ones = ones_ref[...]
    c = _dot(tri, l_hi) + _dot(tri, l_mid) + _dot(tri, l_lo)
    cl = _dot(ones, l_hi) + _dot(ones, l_mid) + _dot(ones, l_lo)
    e_c = jnp.exp(c)
    e_prev = jnp.exp(c - logd)
    e_inv = jnp.exp(-c)
    e_rest = jnp.exp(cl - c)
    rt_ref[...] = (r * e_c).astype(BF16)
    kap_ref[...] = (kk * e_prev).astype(BF16)
    bt_ref[...] = (b * e_inv).astype(BF16)
    kt_ref[...] = (kmod * e_inv).astype(BF16)
    bh_ref[...] = (b * e_rest).astype(BF16)
    kh_ref[...] = (kmod * e_rest).astype(BF16)
    v_ref[...] = v.astype(BF16)
    bonus_ref[...] = bonus.astype(BF16)
    g_ref[...] = g.astype(BF16)
    e_cl = jnp.exp(cl)
    pl8_ref[...] = jnp.concatenate(
        [e_cl[j * CHUNK:j * CHUNK + SUBLANES] for j in range(tm // CHUNK)], axis=0)


def rwkv_prep(z_rw, batch, seq, mu, vecs, lora_w, vres, tm=256):
    t = z_rw.shape[0]
    nb = seq // tm
    has_vres = vres is not None
    ii = lax.broadcasted_iota(jnp.int32, (GROUP_W, GROUP_W), 0)
    jj = lax.broadcasted_iota(jnp.int32, (GROUP_W, GROUP_W), 1)
    bd = (ii // RWKV_HEAD == jj // RWKV_HEAD).astype(BF16)
    it = lax.broadcasted_iota(jnp.int32, (tm, tm), 0)
    jt = lax.broadcasted_iota(jnp.int32, (tm, tm), 1)
    same = it // CHUNK == jt // CHUNK
    tri = (same & (jt <= it)).astype(BF16)
    ones = same.astype(BF16)

    zrow = pl.BlockSpec((tm, RW_PAD), lambda b, i: (b * nb + i, 0))
    zprev = pl.BlockSpec(
        (SUBLANES, RW_PAD), lambda b, i: (jnp.maximum((b * nb + i) * (tm // SUBLANES) - 1, 0), 0))
    full = lambda shp: pl.BlockSpec(shp, lambda b, i: (0,) * len(shp))
    wrow = pl.BlockSpec((tm, RWKV_W), lambda b, i: (b * nb + i, 0))
    in_specs = [zrow, zprev, full((1, RW_PAD)), full((SUBLANES, RWKV_W)),
                full((LANES, 2 * RWKV_W)), full((LANES, 2 * RWKV_W)),
                full((LORA_PAD - LANES, RWKV_W)), full((LORA_PAD - LANES, RWKV_W)),
                full((GROUP_W, GROUP_W)), full((tm, tm)), full((tm, tm))]
    args = [z_rw, z_rw, mu, vecs, *lora_w, bd, tri, ones]
    if has_vres:
        v_first, v0, v1, v2 = vres
        in_specs += [wrow, full((1, RWKV_W)), full((RWKV_W, LANES)), full((LANES, RWKV_W))]
        args += [v_first, v0, v1, v2]
    p8 = tm // CHUNK * SUBLANES
    out_specs = [wrow] * 9 + [pl.BlockSpec((p8, RWKV_W), lambda b, i: (b * nb + i, 0))]
    out_shape = [jax.ShapeDtypeStruct((t, RWKV_W), BF16)] * 9 + [
        jax.ShapeDtypeStruct((t // CHUNK * SUBLANES, RWKV_W), F32)]
    return pl.pallas_call(
        functools.partial(_rwkv_prep_kernel, has_vres),
        grid=(batch, nb), in_specs=in_specs, out_specs=out_specs, out_shape=out_shape,
        compiler_params=_cparams(("parallel", "parallel")), name="rwkv_prep",
    )(*args)


def _rwkv_chunk_kernel(rt_ref, kap_ref, bt_ref, kt_ref, bh_ref, kh_ref, v_ref, pl8_ref,
                       bonus_ref, g_ref, lng_ref, lnb_ref, y_ref, h_ref):
    nbat, tb = rt_ref.shape[0], rt_ref.shape[1]
    w = GROUP_W

    @pl.when(pl.program_id(1) == 0)
    def _():
        h_ref[...] = jnp.zeros_like(h_ref)

    ri = lax.broadcasted_iota(jnp.int32, (w, w), 0)
    ci = lax.broadcasted_iota(jnp.int32, (w, w), 1)
    bd_mask = (ri >> 6) == (ci >> 6)
    eye = ri == ci
    tr = lax.broadcasted_iota(jnp.int32, (CHUNK, w), 0)
    sc = lax.broadcasted_iota(jnp.int32, (CHUNK, w), 1) & (CHUNK - 1)
    strict = sc < tr
    incl = sc <= tr
    seg_mean = jnp.where(bd_mask, 1.0 / RWKV_HEAD, 0.0).astype(BF16)

    def bd(y):
        return jnp.where(bd_mask, jnp.concatenate([y] * (w // CHUNK), axis=0), jnp.zeros((), y.dtype))

    def mm(x, y_bd):
        return _dot(x.astype(BF16), y_bd)

    nch = tb // CHUNK
    cs = range(nbat * nch)
    rows = [(c // nch, slice((c % nch) * CHUNK, (c % nch + 1) * CHUNK)) for c in cs]
    rt = [rt_ref[b, r, :] for b, r in rows]
    kap = [kap_ref[b, r, :] for b, r in rows]
    v = [v_ref[b, r, :] for b, r in rows]
    bh = [bh_ref[b, r, :] for b, r in rows]
    bt_in = [bt_ref[b, r, :] for b, r in rows]
    kt_in = [kt_ref[b, r, :] for b, r in rows]
    kh = [kh_ref[b, r, :] for b, r in rows]
    pl_row = [pl8_ref[c // nch, (c % nch) * SUBLANES:(c % nch) * SUBLANES + 1, :] for c in cs]
    lhs = [jnp.concatenate([kap[c], rt[c]], axis=0) for c in cs]
    a_b = [_dot_nt(lhs[c], bd(bt_in[c])) for c in cs]
    a_k = [_dot_nt(lhs[c], bd(kt_in[c])) for c in cs]
    a_rb = [jnp.where(incl, a_b[c][CHUNK:], 0.0).astype(BF16) for c in cs]
    a_kk = [jnp.concatenate([jnp.where(strict, a_k[c][:CHUNK], 0.0),
                             jnp.where(incl, a_k[c][CHUNK:], 0.0)], axis=0) for c in cs]

    n_pow = [-jnp.where(strict, a_b[c][:CHUNK], 0.0) for c in cs]
    t_inv = [jnp.where(sc == tr, 1.0, 0.0) + n_pow[c] for c in cs]
    n_pow = [mm(n_pow[c], bd(n_pow[c].astype(BF16))) for c in cs]
    for _ in range(4):
        both = [mm(jnp.concatenate([n_pow[c], t_inv[c]], axis=0), bd(n_pow[c].astype(BF16))) for c in cs]
        n_pow = [both[c][:CHUNK] for c in cs]
        t_inv = [t_inv[c] + both[c][CHUNK:] for c in cs]
    t_inv = [t_inv[c] + mm(t_inv[c], bd(n_pow[c].astype(BF16))) for c in cs]

    av = [mm(a_kk[c], bd(v[c])) for c in cs]
    t_bf = [t_inv[c].astype(BF16) for c in cs]
    tk = [_dot(t_bf[c], jnp.concatenate([bd(kap[c]), bd(av[c][:CHUNK].astype(BF16))], axis=1)).astype(BF16)
          for c in cs]
    ar = [_dot(a_rb[c], jnp.concatenate([bd(tk[c][:, :w]), bd(tk[c][:, w:])], axis=1)) for c in cs]
    r_hat = [(rt[c].astype(F32) - ar[c][:, :w]).astype(BF16) for c in cs]
    y_in = [av[c][CHUNK:] - ar[c][:, w:] for c in cs]
    lhs2 = [jnp.concatenate([r_hat[c], tk[c][:, :w]], axis=0) for c in cs]
    s_t = [jnp.concatenate([kh[c], bh[c]], axis=0).astype(F32).T.astype(BF16) for c in cs]
    pl_col = [jnp.sum(jnp.where(eye, pl_row[c], 0.0), axis=1, keepdims=True) for c in cs]

    h = [h_ref[b] for b in range(nbat)]
    y = [None] * len(cs)
    for ch in range(nch):
        for b in range(nbat):
            c = b * nch + ch
            h_hi, h_lo = _split2(h[b])
            uy = _dot(lhs2[c], h_hi) + _dot(lhs2[c], h_lo)
            y[c] = y_in[c] + uy[:CHUNK]
            r_mat = jnp.concatenate([v[c], (-(tk[c][:, w:].astype(F32) + uy[CHUNK:])).astype(BF16)], axis=0)
            h[b] = h[b] * pl_col[c] + jnp.where(bd_mask, _dot(s_t[c], r_mat), 0.0)
    for b in range(nbat):
        h_ref[b] = h[b]

    y_all = jnp.concatenate(y, axis=0)
    mu = _dot_x2(y_all, seg_mean)
    yc = y_all - mu
    var = _dot_x2(yc * yc, seg_mean)
    out = yc * lax.rsqrt(var + RWKV_GN_EPS) * lng_ref[...] + lnb_ref[...]
    for b in range(nbat):
        o = (out[b * tb:(b + 1) * tb] + bonus_ref[b].astype(F32)) * g_ref[b].astype(F32)
        y_ref[b] = o.astype(y_ref.dtype)


def rwkv_chunks(prep, batch, seq, lnx_g, lnx_b, tb=512):
    rt, kap, bt, kt, bh, kh, v, bonus, g, pl8 = prep
    nb = seq // tb
    ng = RWKV_W // GROUP_W
    p8_rows = tb // CHUNK * SUBLANES
    seq3 = lambda a: a.reshape(batch, a.shape[0] // batch, RWKV_W)
    blk = pl.BlockSpec((batch, tb, GROUP_W), lambda q, i: (0, i, q))
    p8 = pl.BlockSpec((batch, p8_rows, GROUP_W), lambda q, i: (0, i, q))
    vec = pl.BlockSpec((1, GROUP_W), lambda q, i: (0, q))
    y = pl.pallas_call(
        _rwkv_chunk_kernel,
        grid=(ng, nb),
        in_specs=[blk] * 7 + [p8, blk, blk, vec, vec],
        out_specs=blk,
        out_shape=jax.ShapeDtypeStruct((batch, seq, RWKV_W), BF16),
        scratch_shapes=[pltpu.VMEM((batch, GROUP_W, GROUP_W), F32)],
        compiler_params=_cparams(("parallel", "arbitrary")), name="rwkv_chunks",
    )(*[seq3(a) for a in (rt, kap, bt, kt, bh, kh, v, pl8, bonus, g)],
      lnx_g.reshape(1, RWKV_W), lnx_b.reshape(1, RWKV_W))
    return y.reshape(batch * seq, RWKV_W)


def _mla_prep_kernel(zq_ref, zkv_ref, cos_ref, sin_ref, qn_ref, kvn_ref, wq_ref,
                     wk_ref, wv_ref, e_ref, q_ref, k_ref, v_ref):
    zq = zq_ref[...].astype(F32)
    qn = (zq * lax.rsqrt(jnp.mean(zq * zq, axis=-1, keepdims=True) + RMS_EPS) * qn_ref[...]).astype(BF16)
    zkv = zkv_ref[:, 0:KV_LORA].astype(F32)
    cn = (zkv * lax.rsqrt(jnp.mean(zkv * zkv, axis=-1, keepdims=True) + RMS_EPS) * kvn_ref[...]).astype(BF16)
    kr = zkv_ref[:, KV_LORA:KVR_PAD]
    cos = cos_ref[...]
    sin = sin_ref[...]
    q_all = _dot(qn, wq_ref[...])
    k_all = _dot(cn, wk_ref[...]) + _dot(kr, e_ref[...])

    def rotary(t2):
        return t2 * cos + pltpu.roll(t2, QK_ROPE, 1) * sin

    for h in range(MLA_HEADS):
        nope = slice(h * QK_PAD, h * QK_PAD + QK_NOPE)
        rope = slice(h * QK_PAD + QK_NOPE, (h + 1) * QK_PAD)
        q_ref[:, nope] = (q_all[:, nope] * ATTN_SCALE).astype(BF16)
        q_ref[:, rope] = (rotary(q_all[:, rope]) * ATTN_SCALE).astype(BF16)
        k_ref[:, nope] = k_all[:, nope].astype(BF16)
        k_ref[:, rope] = rotary(k_all[:, rope]).astype(BF16)
    v_ref[...] = _dot_nt(wv_ref[...], cn).astype(BF16)


def mla_prep(z_q, z_kvr, cos_t, sin_t, qn, kvn, wq, wk, wv_t, e_mat, tm=VT_TILE):
    t = z_q.shape[0]
    hw = MLA_HEADS * QK_PAD
    vw = MLA_HEADS * V_HEAD
    row = lambda w: pl.BlockSpec((tm, w), lambda i: (i, 0))
    full = lambda a: pl.BlockSpec(a.shape, lambda i: (0, 0))
    return pl.pallas_call(
        _mla_prep_kernel, grid=(t // tm,),
        in_specs=[row(Q_LORA), row(KVR_PAD), row(LANES), row(LANES), full(qn), full(kvn),
                  full(wq), full(wk), full(wv_t), full(e_mat)],
        out_specs=[row(hw), row(hw), pl.BlockSpec((None, vw, tm), lambda i: (i, 0, 0))],
        out_shape=[jax.ShapeDtypeStruct((t, hw), BF16), jax.ShapeDtypeStruct((t, hw), BF16),
                   jax.ShapeDtypeStruct((t // tm, vw, tm), BF16)],
        compiler_params=_cparams(("parallel",)), name="mla_prep",
    )(z_q, z_kvr, cos_t, sin_t, qn, kvn, wq, wk, wv_t, e_mat)


def _attn_kernel(q_ref, k_ref, vt_ref, o_ref, st0_ref, st1_ref, mx0_ref, mx1_ref, m_ref, acc_ref):
    tq = q_ref.shape[0]
    tk = tq
    i = pl.program_id(2)
    q = q_ref[...]
    ones = jnp.ones((BF16_ROWS, tk), BF16)

    def qk(j, st_ref, mx_ref):
        start = pl.multiple_of(j * tk, tk)
        st = _dot_nt(k_ref[pl.ds(start, tk), :], q)
        st_ref[...] = st
        mx_ref[...] = jnp.max(st, axis=0, keepdims=True)

    def consume(j, st_ref, mx_ref, masked):
        st = st_ref[...]
        if masked:
            ki = lax.broadcasted_iota(jnp.int32, st.shape, 0)
            qi = lax.broadcasted_iota(jnp.int32, st.shape, 1)
            st = jnp.where(ki <= qi, st, NEG_BIG)
            mx = jnp.max(st, axis=0, keepdims=True)
        else:
            mx = mx_ref[...]
        m = m_ref[...]
        m_new = jnp.maximum(m, mx)
        alpha = jnp.exp(m - m_new)
        p = jnp.exp(st - m_new).astype(BF16)
        per = tk // VT_TILE
        vt = jnp.concatenate([jnp.concatenate([vt_ref[per * j + s] for s in range(per)], axis=1), ones],
                             axis=0)
        acc_ref[...] = alpha * acc_ref[...] + _dot(vt, p)
        m_ref[...] = m_new

    m_ref[...] = jnp.full(m_ref.shape, NEG_BIG, F32)
    acc_ref[...] = jnp.zeros(acc_ref.shape, F32)
    qk(0, st0_ref, mx0_ref)

    def pair(jj, carry):
        j = 2 * jj
        qk(j + 1, st1_ref, mx1_ref)
        consume(j, st0_ref, mx0_ref, False)
        qk(j + 2, st0_ref, mx0_ref)
        consume(j + 1, st1_ref, mx1_ref, False)
        return carry

    lax.fori_loop(0, i // 2, pair, 0)

    @pl.when(i % 2 == 0)
    def _():
        consume(i, st0_ref, mx0_ref, True)

    @pl.when(i % 2 == 1)
    def _():
        qk(i, st1_ref, mx1_ref)
        consume(i - 1, st0_ref, mx0_ref, False)
        consume(i, st1_ref, mx1_ref, True)

    acc = acc_ref[...]
    out = acc[:V_HEAD] / acc[V_HEAD:V_HEAD + 1]
    o_ref[...] = out.T.astype(o_ref.dtype)


def attention(q_all, k_all, vt_all, batch, seq, tq=ATTN_TILE):
    t = q_all.shape[0]
    nq = seq // tq
    return pl.pallas_call(
        _attn_kernel,
        grid=(batch, MLA_HEADS, nq),
        in_specs=[pl.BlockSpec((tq, QK_PAD), lambda b, h, i: (b * nq + i, h)),
                  pl.BlockSpec((seq, QK_PAD), lambda b, h, i: (b, h)),
                  pl.BlockSpec((seq // VT_TILE, V_HEAD, VT_TILE), lambda b, h, i: (b, h, 0))],
        out_specs=pl.BlockSpec((tq, V_HEAD), lambda b, h, i: (b * nq + i, h)),
        out_shape=jax.ShapeDtypeStruct((t, MLA_HEADS * V_HEAD), BF16),
        scratch_shapes=[pltpu.VMEM((tq, tq), F32), pltpu.VMEM((tq, tq), F32),
                        pltpu.VMEM((1, tq), F32), pltpu.VMEM((1, tq), F32),
                        pltpu.VMEM((1, tq), F32), pltpu.VMEM((V_HEAD + BF16_ROWS, tq), F32)],
        compiler_params=_cparams(("parallel", "parallel", "arbitrary")), name="attention",
    )(q_all, k_all, vt_all)


CONV_HALO = 32


def _conv_kernel(z_ref, zh_ref, dw_ref, db_ref, g_ref, b_ref, o_ref, u_ref, s_ref):
    tm = z_ref.shape[0]
    first = pl.program_id(1) == 0

    def glu(zz):
        zz = zz.astype(F32)
        return zz[:, :CONV_W] * _sigmoid(zz[:, CONV_W:])

    u_ref[0:CONV_HALO, :] = jnp.where(first, 0.0, glu(zh_ref[...]))
    u_ref[CONV_HALO:, :] = glu(z_ref[...])
    off = CONV_HALO - (CONV_K - 1)
    acc = jnp.zeros((tm, CONV_W), F32) + db_ref[...]
    for res in range(SUBLANES):
        taps = [j for j in range(CONV_K) if (off + j) % SUBLANES == res]
        span = max(off + j for j in taps) - res + tm
        if res == 0:
            src_ref = u_ref
        else:
            s_ref[0:span, :] = u_ref[res:res + span, :]
            src_ref = s_ref
        for j in taps:
            lo = off + j - res
            acc = acc + dw_ref[j:j + 1, :] * src_ref[lo:lo + tm, :]
    y = _ln_rows(acc, g_ref[...], b_ref[...])
    o_ref[...] = (y * _sigmoid(y)).astype(o_ref.dtype)


def conformer_conv(z_conv, batch, seq, dw, db, ln_g, ln_b, tm=256):
    t = z_conv.shape[0]
    nb = seq // tm
    ratio = tm // CONV_HALO
    vec = pl.BlockSpec((1, CONV_W), lambda b, i: (0, 0))
    return pl.pallas_call(
        _conv_kernel, grid=(batch, nb),
        in_specs=[pl.BlockSpec((tm, 2 * CONV_W), lambda b, i: (b * nb + i, 0)),
                  pl.BlockSpec((CONV_HALO, 2 * CONV_W),
                               lambda b, i: (jnp.maximum((b * nb + i) * ratio - 1, 0), 0)),
                  pl.BlockSpec((CONV_HALO, CONV_W), lambda b, i: (0, 0)), vec, vec, vec],
        out_specs=pl.BlockSpec((tm, CONV_W), lambda b, i: (b * nb + i, 0)),
        out_shape=jax.ShapeDtypeStruct((t, CONV_W), BF16),
        scratch_shapes=[pltpu.VMEM((tm + CONV_HALO, CONV_W), F32), pltpu.VMEM((tm + CONV_HALO, CONV_W), F32)],
        compiler_params=_cparams(("parallel", "parallel")), name="conformer_conv",
    )(z_conv, z_conv, dw, db.reshape(1, CONV_W), ln_g.reshape(1, CONV_W), ln_b.reshape(1, CONV_W))


def _merge_kernel(ya_ref, yb_ref, yc_ref, ga_ref, gb_ref, gc_ref, wb_ref, o_ref):
    acc = None
    for i, (y_ref, zg_ref) in enumerate(((ya_ref, ga_ref), (yb_ref, gb_ref), (yc_ref, gc_ref))):
        term = _sigmoid(zg_ref[...].astype(F32)) * _dot(y_ref[...], wb_ref[i])
        acc = term if acc is None else acc + term
    o_ref[...] = acc.astype(o_ref.dtype)


def branch_merge(y_a, y_b, y_c, zg, wb, tm=512, tn=1024):
    t = y_a.shape[0]
    yspec = pl.BlockSpec((tm, RWKV_W), lambda j, i: (i, 0))
    gspec = pl.BlockSpec((tm, tn), lambda j, i: (i, j))
    return pl.pallas_call(
        _merge_kernel, grid=(D_MODEL // tn, t // tm),
        in_specs=[yspec, yspec, yspec, gspec, gspec, gspec,
                  pl.BlockSpec((N_BRANCH, RWKV_W, tn), lambda j, i: (0, 0, j))],
        out_specs=pl.BlockSpec((tm, tn), lambda j, i: (i, j)),
        out_shape=jax.ShapeDtypeStruct((t, D_MODEL), BF16),
        compiler_params=_cparams(("parallel", "parallel")), name="branch_merge",
    )(y_a, y_b, y_c, *zg, wb)


def _out_ln_router_kernel(m_ref, w_ref, h_ref, g_ref, b_ref, rw_hi_ref, rw_lo_ref, rb_ref, tri_ref,
                          ho_ref, hb_ref, ei_ref, gw_ref, cnt_out_ref, cnt_ref):
    x = DN_ALPHA * h_ref[...] + _dot(m_ref[...], w_ref[...])
    h = _ln_rows(x, g_ref[...], b_ref[...])
    ho_ref[...] = h
    hb_ref[...] = h.astype(BF16)
    h_hi, h_lo = _split2(h)
    rw_hi = rw_hi_ref[...]
    logits = _dot_nt(rw_hi, h_hi) + _dot_nt(rw_hi, h_lo) + _dot_nt(rw_lo_ref[...], h_hi)
    scores = _sigmoid(logits)
    sel = scores + rb_ref[...]
    tm = sel.shape[1]
    idx = lax.broadcasted_iota(jnp.int32, (EXP_PER_GROUP, tm), 0)

    def first_max(x):
        mx = jnp.max(x, axis=0, keepdims=True)
        return mx, jnp.min(jnp.where(x == mx, idx, EXP_PER_GROUP), axis=0, keepdims=True)

    best = None
    for grp in range(N_GROUPS):
        rows = slice(grp * EXP_PER_GROUP, (grp + 1) * EXP_PER_GROUP)
        x = sel[rows]
        sc_g = scores[rows]
        m1, i1 = first_max(x)
        m2, i2 = first_max(jnp.where(idx == i1, -jnp.inf, x))
        s1 = jnp.sum(jnp.where(idx == i1, sc_g, 0.0), axis=0, keepdims=True)
        s2 = jnp.sum(jnp.where(idx == i2, sc_g, 0.0), axis=0, keepdims=True)
        cand = (m1 + m2, i1 + grp * EXP_PER_GROUP, i2 + grp * EXP_PER_GROUP, s1, s2)
        if best is None:
            best = cand
        else:
            take = cand[0] > best[0]
            best = tuple(jnp.where(take, cn, bs) for cn, bs in zip(cand, best))
    _, e1, e2, s1, s2 = best
    @pl.when(pl.program_id(0) == 0)
    def _():
        cnt_ref[...] = jnp.zeros_like(cnt_ref)

    erow = lax.broadcasted_iota(jnp.int32, (N_EXPERTS, tm), 0)
    hit1 = erow == e1
    hit2 = erow == e2
    onehot = jnp.where(hit1, 1.0, jnp.where(hit2, 1.0, 0.0))
    before = cnt_ref[:, 0:1] + _dot(onehot.astype(BF16), tri_ref[...]) - onehot
    rank1 = jnp.sum(jnp.where(hit1, before, 0.0), axis=0, keepdims=True).astype(jnp.int32)
    rank2 = jnp.sum(jnp.where(hit2, before, 0.0), axis=0, keepdims=True).astype(jnp.int32)
    cnt_ref[...] = cnt_ref[...] + jnp.sum(onehot, axis=1, keepdims=True)
    cnt_out_ref[...] = cnt_ref[...]
    row = lax.broadcasted_iota(jnp.int32, (SUBLANES, tm), 0)
    ei_ref[...] = jnp.where(row == 0, e1, jnp.where(row == 1, e2,
                                                    jnp.where(row == 2, rank1, jnp.where(row == 3, rank2, 0))))
    inv = 1.0 / (s1 + s2)
    gw_ref[...] = jnp.where(row == 0, s1 * inv, jnp.where(row == 1, s2 * inv, 0.0))


def out_ln_router(merged, w_out, h, g, b, rw_hi, rw_lo, rb, tm=512):
    t, d = h.shape
    si = lax.broadcasted_iota(jnp.int32, (tm, tm), 0)
    ti = lax.broadcasted_iota(jnp.int32, (tm, tm), 1)
    tri = (si <= ti).astype(BF16)
    row = pl.BlockSpec((tm, d), lambda i: (i, 0))
    vec = pl.BlockSpec((1, d), lambda i: (0, 0))
    rws = pl.BlockSpec((N_EXPERTS, d), lambda i: (0, 0))
    tok = pl.BlockSpec((SUBLANES, tm), lambda i: (0, i))
    return pl.pallas_call(
        _out_ln_router_kernel, grid=(t // tm,),
        in_specs=[row, pl.BlockSpec((d, d), lambda i: (0, 0)), row, vec, vec, rws, rws,
                  pl.BlockSpec((N_EXPERTS, tm), lambda i: (0, 0)), pl.BlockSpec((tm, tm), lambda i: (0, 0))],
        out_specs=[row, row, tok, tok, pl.BlockSpec((N_EXPERTS, LANES), lambda i: (0, 0))],
        out_shape=[jax.ShapeDtypeStruct((t, d), F32), jax.ShapeDtypeStruct((t, d), BF16),
                   jax.ShapeDtypeStruct((SUBLANES, t), jnp.int32), jax.ShapeDtypeStruct((SUBLANES, t), F32),
                   jax.ShapeDtypeStruct((N_EXPERTS, LANES), F32)],
        scratch_shapes=[pltpu.VMEM((N_EXPERTS, LANES), F32)],
        compiler_params=_cparams(("arbitrary",)), name="out_ln_router",
    )(merged, w_out, h, g.reshape(1, d), b.reshape(1, d), rw_hi, rw_lo,
      jnp.broadcast_to(rb.reshape(N_EXPERTS, 1), (N_EXPERTS, tm)), tri)


def _moe_kernel(be_ref, nv_ref, x_ref, w13_ref, w2_ref, o_ref, w13b_ref, w2b_ref):
    i = pl.program_id(0)

    @pl.when((i == 0) | (be_ref[i] != be_ref[jnp.maximum(i - 1, 0)]))
    def _():
        w13b_ref[...] = w13_ref[...].astype(BF16)
        w2b_ref[...] = w2_ref[...].astype(BF16)

    @pl.when(i < nv_ref[0])
    def _():
        gu = _dot(x_ref[...], w13b_ref[...])
        gate = gu[:, :D_EXPERT]
        act = gate * _sigmoid(gate) * gu[:, D_EXPERT:]
        o_ref[...] = _dot(act.astype(BF16), w2b_ref[...]).astype(o_ref.dtype)

    @pl.when(i >= nv_ref[0])
    def _():
        o_ref[...] = jnp.zeros_like(o_ref)


def moe_experts(x_sorted, block_e, n_valid, w13, w2, layer):
    p, d = x_sorted.shape
    nblk = p // MOE_TB
    grid_spec = pltpu.PrefetchScalarGridSpec(
        num_scalar_prefetch=2, grid=(nblk,),
        in_specs=[pl.BlockSpec((MOE_TB, d), lambda i, be, nv: (i, 0)),
                  pl.BlockSpec((None, None, d, 2 * D_EXPERT), lambda i, be, nv: (layer, be[i], 0, 0)),
                  pl.BlockSpec((None, None, D_EXPERT, d), lambda i, be, nv: (layer, be[i], 0, 0))],
        out_specs=pl.BlockSpec((MOE_TB, d), lambda i, be, nv: (i, 0)),
        scratch_shapes=[pltpu.VMEM((d, 2 * D_EXPERT), BF16), pltpu.VMEM((D_EXPERT, d), BF16)])
    return pl.pallas_call(
        _moe_kernel, grid_spec=grid_spec,
        out_shape=jax.ShapeDtypeStruct((p, d), BF16),
        compiler_params=_cparams(("arbitrary",)), name="moe_experts",
    )(block_e, n_valid, x_sorted, w13, w2)


def _combine_ln_kernel(final, h_ref, y0_ref, y1_ref, gw_ref, g_ref, b_ref, *outs):
    gw = gw_ref[...]
    f = gw[:, 0:1] * y0_ref[...].astype(F32) + gw[:, 1:2] * y1_ref[...].astype(F32)
    h = _ln_rows(DN_ALPHA * h_ref[...] + f, g_ref[...], b_ref[...])
    outs[0][...] = h
    if not final:
        outs[1][...] = h.astype(BF16)


def combine_ln(h, y0, y1, gate_w, g, b, final, tm=256):
    t, d = h.shape
    row = pl.BlockSpec((tm, d), lambda i: (i, 0))
    vec = pl.BlockSpec((1, d), lambda i: (0, 0))
    out_specs = [row] if final else [row, row]
    out_shape = [jax.ShapeDtypeStruct((t, d), F32)] + ([] if final else [jax.ShapeDtypeStruct((t, d), BF16)])
    return pl.pallas_call(
        functools.partial(_combine_ln_kernel, final), grid=(t // tm,),
        in_specs=[row, row, row, pl.BlockSpec((tm, LANES), lambda i: (i, 0)), vec, vec],
        out_specs=out_specs, out_shape=out_shape,
        compiler_params=_cparams(("parallel",)), name="combine_ln",
    )(h, y0, y1, gate_w, g.reshape(1, d), b.reshape(1, d))


def _route(e_rows, gate_rows, counts):
    e_idx = e_rows[:TOP_K].T
    rank = e_rows[TOP_K:2 * TOP_K].T.reshape(-1)
    gate = gate_rows[:TOP_K].T
    t = e_idx.shape[0]
    a = t * TOP_K
    flat_e = e_idx.reshape(a)
    counts = counts[:, 0].astype(jnp.int32)
    padded = (counts + MOE_TB - 1) // MOE_TB * MOE_TB
    ends = jnp.cumsum(padded)
    pstarts = ends - padded
    e_hot = flat_e[:, None] == jnp.arange(N_EXPERTS, dtype=jnp.int32)[None, :]
    dest = jnp.sum(jnp.where(e_hot, pstarts[None, :], 0), axis=1) + rank
    p = a + N_EXPERTS * MOE_TB
    nblk = p // MOE_TB
    flat_tok = jnp.repeat(jnp.arange(t, dtype=jnp.int32), TOP_K)
    tok_pad = (jnp.arange(p, dtype=jnp.int32) % t).at[dest].set(flat_tok)
    block_start = jnp.arange(nblk, dtype=jnp.int32) * MOE_TB
    block_e = jnp.minimum(jnp.sum((ends[None, :] <= block_start[:, None]).astype(jnp.int32), axis=1),
                          N_EXPERTS - 1)
    n_valid = (ends[-1] // MOE_TB).astype(jnp.int32).reshape(1)
    dest2 = dest.reshape(t, TOP_K)
    gate_w = jnp.pad(gate, ((0, 0), (0, LANES - TOP_K)))
    return tok_pad, block_e, n_valid, dest2[:, 0], dest2[:, 1], gate_w


def _pad_cols(w, n):
    return jnp.pad(w, ((0, 0), (0, n - w.shape[1])))


def _pad_rows(w, n):
    return jnp.pad(w, ((0, n - w.shape[0]), (0, 0)))


def _rope_tables(positions):
    inv = ROPE_THETA ** (-jnp.arange(0, QK_ROPE, 2, dtype=F32) / QK_ROPE)
    ang = positions.astype(F32).reshape(-1)[:, None] * inv
    cos, sin = jnp.cos(ang), jnp.sin(ang)
    zeros = jnp.zeros((ang.shape[0], LANES - QK_ROPE), F32)
    return jnp.concatenate([cos, cos, zeros], axis=1), jnp.concatenate([sin, sin, zeros], axis=1)


def _rot_cols(w):
    half = QK_ROPE // 2
    return jnp.concatenate([-w[..., half:], w[..., :half]], axis=-1)


def _mla_weights(q_b, kv_b):
    qb = q_b.reshape(Q_LORA, MLA_HEADS, QK_NOPE + QK_ROPE)
    wq = jnp.concatenate([qb, _rot_cols(qb[..., QK_NOPE:])], axis=-1).reshape(Q_LORA, MLA_HEADS * QK_PAD)
    kvb = kv_b.reshape(KV_LORA, MLA_HEADS, QK_NOPE + V_HEAD)
    wk = jnp.concatenate([kvb[..., :QK_NOPE], jnp.zeros((KV_LORA, MLA_HEADS, QK_PAD - QK_NOPE), F32)],
                         axis=-1).reshape(KV_LORA, MLA_HEADS * QK_PAD)
    wv = kvb[..., QK_NOPE:].reshape(KV_LORA, MLA_HEADS * V_HEAD).T
    eye = jnp.eye(QK_ROPE, dtype=F32)
    e_mat = _pad_rows(jnp.tile(jnp.concatenate([jnp.zeros((QK_ROPE, QK_NOPE), F32), eye, _rot_cols(eye)],
                                               axis=1), (1, MLA_HEADS)), KVR_PAD - KV_LORA)
    return tuple(m.astype(BF16) for m in (wq, wk, wv, e_mat))


def kernel(x, positions, ln0_g, ln0_b, w_in, rw_mu, rw_w0, rw_w2, rw_a0, rw_a2, rw_g2, rw_kk, rw_ka, rw_rk, rw_lnx_g, rw_lnx_b, rw_v0, rw_v1, rw_v2, mla_q_norm, mla_q_b, mla_kv_norm, mla_kv_b, conv_dw, conv_db, conv_ln_g, conv_ln_b, w_branch, w_out, ln1_g, ln1_b, router_w, router_b, moe_w13, moe_w2, ln2_g, ln2_b):
    batch, seq, d = x.shape
    t = batch * seq
    h, hb = layer_norm0(x.reshape(t, d), ln0_g, ln0_b)
    cos_t, sin_t = _rope_tables(positions)
    rw_t = router_w.T
    rw_hi = rw_t.astype(BF16)
    rw_lo = (rw_t - rw_hi.astype(F32)).astype(BF16)
    o_q = RW_SHIFT
    o_kv = o_q + Q_LORA
    o_conv = o_kv + KV_LORA + QK_ROPE
    o_gate = o_conv + 2 * CONV_W
    v_first = None
    for l in range(DEPTH):
        wi = w_in[l]
        w_rw = _pad_cols(wi[:, :RW_SHIFT], RW_PAD).astype(BF16)
        w_q = wi[:, o_q:o_kv].astype(BF16)
        w_kvr = _pad_cols(wi[:, o_kv:o_conv], KVR_PAD).astype(BF16)
        w_conv = wi[:, o_conv:o_gate].astype(BF16)
        w_gate = wi[:, o_gate:].astype(BF16)

        z_rw = matmul(hb, w_rw, 512, RW_PAD // 3)
        mu = _pad_cols(rw_mu[l].reshape(1, RW_SHIFT), RW_PAD)
        vecs = _pad_rows(jnp.stack([rw_w0[l], rw_a0[l], rw_kk[l], rw_ka[l], rw_rk[l]]), SUBLANES)
        zw2 = jnp.zeros((DECAY_LORA, RWKV_W), F32)
        w_wa = jnp.concatenate([jnp.concatenate([rw_w2[l], zw2], axis=1),
                                jnp.concatenate([zw2, rw_a2[l]], axis=1)], axis=0)
        w_g = _pad_rows(rw_g2[l], LORA_PAD - LANES)
        lora_w = []
        for wmat in (w_wa, w_g):
            w_hi = wmat.astype(BF16)
            lora_w += [w_hi, (wmat - w_hi.astype(F32)).astype(BF16)]
        if l == 0:
            vres = None
        else:
            vres = (v_first, rw_v0[l - 1].reshape(1, RWKV_W),
                    _pad_cols(rw_v1[l - 1], LANES).astype(BF16), _pad_rows(rw_v2[l - 1], LANES).astype(BF16))
        prep = rwkv_prep(z_rw, batch, seq, mu, vecs, lora_w, vres)
        if l == 0:
            v_first = prep[6]
        y_a = rwkv_chunks(prep, batch, seq, rw_lnx_g[l], rw_lnx_b[l])

        z_q = matmul(hb, w_q, 512, Q_LORA)
        z_kvr = matmul(hb, w_kvr, 512, KVR_PAD)
        mw = _mla_weights(mla_q_b[l], mla_kv_b[l])
        q_all, k_all, v_all = mla_prep(z_q, z_kvr, cos_t, sin_t, mla_q_norm[l].reshape(1, Q_LORA),
                                       mla_kv_norm[l].reshape(1, KV_LORA), *mw)
        y_b = attention(q_all, k_all, v_all, batch, seq)

        z_conv = matmul(hb, w_conv, 512, CONV_W)
        y_c = conformer_conv(z_conv, batch, seq, _pad_rows(conv_dw[l], CONV_HALO), conv_db[l],
                             conv_ln_g[l], conv_ln_b[l])

        z_gate = [matmul(hb, w_gate[:, i * d:(i + 1) * d], 512, 1024) for i in range(N_BRANCH)]
        merged = branch_merge(y_a, y_b, y_c, z_gate, w_branch[l].astype(BF16))
        h, hb, e_rows, gate_rows, counts = out_ln_router(merged, w_out[l].astype(BF16), h, ln1_g[l], ln1_b[l],
                                                         rw_hi, rw_lo, router_b)

        tok_pad, block_e, n_valid, p0, p1, gate_w = _route(e_rows, gate_rows, counts)
        ys = moe_experts(hb[tok_pad], block_e, n_valid, moe_w13, moe_w2, l)
        final = l == DEPTH - 1
        res = combine_ln(h, ys[p0], ys[p1], gate_w, ln2_g[l], ln2_b[l], final)
        if final:
            h = res[0]
        else:
            h, hb = res
    return h.reshape(batch, seq, d)
```
